```python
import math
import jax
import jax.numpy as jnp
from jax import lax
import numpy as np

D_MODEL = 1024
BATCH = 4
SEQ = 8192
DEPTH = 2
DEC_BATCH = 16
DEC_SEQ = 4096
PAST_LEN = 128

GRID_W = 64
CHUNK = 128
D_A = 1024
N_GROUPS_A = 8
GROUP_A = D_A // N_GROUPS_A
N_HEADS_B = 16
HEAD_DIM_B = 64
D_B = N_HEADS_B * HEAD_DIM_B
WIN_H_MAX = 8
WIN_W = 16
RPB_H = 2 * WIN_H_MAX - 1
RPB_W = 2 * WIN_W - 1
D_FF = 2816
D_IN = 2 * D_A + 3 * D_B + 2 * D_MODEL
EPS = 1e-6

kernel_name = "hybrid_gmlp_natten_macaron_encoder"


def rmsnorm(x, g):
    xf = x.astype(jnp.float32)
    y = xf * lax.rsqrt(jnp.mean(xf * xf, axis=-1, keepdims=True) + EPS)
    return (y * g.astype(jnp.float32)).astype(x.dtype)


def layernorm(x, g, b):
    xf = x.astype(jnp.float32)
    mu = jnp.mean(xf, axis=-1, keepdims=True)
    var = jnp.mean(jnp.square(xf - mu), axis=-1, keepdims=True)
    y = (xf - mu) * lax.rsqrt(var + EPS)
    return (y * g.astype(jnp.float32) + b.astype(jnp.float32)).astype(x.dtype)


def swiglu(x, w_gate, w_up, w_down):
    return (jax.nn.silu(x @ w_gate) * (x @ w_up)) @ w_down


def spatial_gating(u, v, ln_g, ln_b, w_s, b_s):
    bsz, s, _ = v.shape
    v = layernorm(v, ln_g, ln_b)
    v = v.reshape(bsz, s // CHUNK, CHUNK, N_GROUPS_A, GROUP_A)
    mixed = jnp.einsum('gpq,bnqgc->bnpgc', w_s, v) + jnp.transpose(b_s)[None, None, :, :, None]
    return u * mixed.reshape(bsz, s, D_A)


def neighbourhood_attention(q, k, v, rpb):
    bsz, s, _ = q.shape
    rows = s // GRID_W
    kh = min(WIN_H_MAX, rows)
    scale = HEAD_DIM_B ** -0.5
    qg = (q * scale).reshape(bsz, rows, GRID_W, N_HEADS_B, HEAD_DIM_B)
    kg = k.reshape(bsz, rows, GRID_W, N_HEADS_B, HEAD_DIM_B)
    vg = v.reshape(bsz, rows, GRID_W, N_HEADS_B, HEAD_DIM_B)

    col = jnp.arange(GRID_W, dtype=jnp.int32)
    col_start = jnp.clip(col - WIN_W // 2, 0, GRID_W - WIN_W)
    col_idx = col_start[:, None] + jnp.arange(WIN_W, dtype=jnp.int32)[None, :]
    col_off = col_idx - col[:, None] + (WIN_W - 1)
    rpb_cols = rpb[:, :, col_off]

    row = jnp.arange(rows, dtype=jnp.int32)
    row_start = jnp.clip(row - kh // 2, 0, rows - kh)
    q_rows = jnp.transpose(qg, (1, 0, 2, 3, 4))

    def one_row(args):
        q_r, r, rs = args
        k_band = lax.dynamic_slice_in_dim(kg, rs, kh, axis=1)
        v_band = lax.dynamic_slice_in_dim(vg, rs, kh, axis=1)
        k_win = k_band[:, :, col_idx]
        v_win = v_band[:, :, col_idx]
        scores = jnp.einsum('bqhd,bkqwhd->bhqkw', q_r, k_win).astype(jnp.float32)
        row_off = rs + jnp.arange(kh, dtype=jnp.int32) - r + (WIN_H_MAX - 1)
        bias = jnp.transpose(rpb_cols[:, row_off], (0, 2, 1, 3))
        scores = scores + bias[None].astype(jnp.float32)
        p = jax.nn.softmax(scores.reshape(bsz, N_HEADS_B, GRID_W, kh * WIN_W), axis=-1)
        p = p.reshape(bsz, N_HEADS_B, GRID_W, kh, WIN_W).astype(v.dtype)
        return jnp.einsum('bhqkw,bkqwhd->bqhd', p, v_win)

    out = lax.map(one_row, (q_rows, row, row_start))
    return jnp.transpose(out, (1, 0, 2, 3, 4)).reshape(bsz, s, D_B)


def trunk(x, ffn1_norm, ffn1_w_gate, ffn1_w_up, ffn1_w_down, mix_norm, w_in, b_gate,
          sgu_ln_g, sgu_ln_b, sgu_w_s, sgu_b_s, nat_rpb, w_branch_a, w_branch_b, w_out,
          ffn2_norm, ffn2_w_gate, ffn2_w_up, ffn2_w_down, final_norm):
    splits = [D_A, 2 * D_A, 2 * D_A + D_B, 2 * D_A + 2 * D_B, 2 * D_A + 3 * D_B,
              2 * D_A + 3 * D_B + D_MODEL]
    h = x
    for l in range(DEPTH):
        h = h + 0.5 * swiglu(rmsnorm(h, ffn1_norm[l]), ffn1_w_gate[l], ffn1_w_up[l], ffn1_w_down[l])
        n = rmsnorm(h, mix_norm[l])
        z = n @ w_in[l]
        u, v, q, k, vv, ga, gb = jnp.split(z, splits, axis=-1)
        ya = spatial_gating(jax.nn.gelu(u, approximate=False), jax.nn.gelu(v, approximate=False),
                            sgu_ln_g[l], sgu_ln_b[l], sgu_w_s[l], sgu_b_s[l])
        yb = neighbourhood_attention(q, k, vv, nat_rpb[l])
        gate_a = jax.nn.sigmoid(ga + b_gate[l, 0])
        gate_b = jax.nn.sigmoid(gb + b_gate[l, 1])
        merged = gate_a * (ya @ w_branch_a[l]) + gate_b * (yb @ w_branch_b[l])
        h = h + merged @ w_out[l]
        h = h + 0.5 * swiglu(rmsnorm(h, ffn2_norm[l]), ffn2_w_gate[l], ffn2_w_up[l], ffn2_w_down[l])
    return rmsnorm(h, final_norm)


def setup_inputs(seed: int = 0) -> dict:
    key = jax.random.key(seed)
    ks = jax.random.split(key, 24)
    f32 = jnp.float32

    def nrm(k, shape, scale):
        return jax.random.normal(k, shape, f32) * scale

    def gain(k, shape):
        return 1.0 + 0.05 * jax.random.normal(k, shape, f32)

    return {
        "x_prompt": jax.random.normal(ks[0], (BATCH, SEQ, D_MODEL), f32),
        "x_sample": jax.random.normal(ks[1], (DEC_BATCH, DEC_SEQ, D_MODEL), f32),
        "ffn1_norm": gain(ks[2], (DEPTH, D_MODEL)),
        "ffn1_w_gate": nrm(ks[3], (DEPTH, D_MODEL, D_FF), D_MODEL ** -0.5),
        "ffn1_w_up": nrm(ks[4], (DEPTH, D_MODEL, D_FF), D_MODEL ** -0.5),
        "ffn1_w_down": nrm(ks[5], (DEPTH, D_FF, D_MODEL), D_FF ** -0.5),
        "mix_norm": gain(ks[6], (DEPTH, D_MODEL)),
        "w_in": nrm(ks[7], (DEPTH, D_MODEL, D_IN), D_MODEL ** -0.5),
        "b_gate": nrm(ks[8], (DEPTH, 2, D_MODEL), 0.01),
        "sgu_ln_g": gain(ks[9], (DEPTH, D_A)),
        "sgu_ln_b": nrm(ks[10], (DEPTH, D_A), 0.02),
        "sgu_w_s": nrm(ks[11], (DEPTH, N_GROUPS_A, CHUNK, CHUNK), CHUNK ** -0.5),
        "sgu_b_s": 1.0 + nrm(ks[12], (DEPTH, N_GROUPS_A, CHUNK), 0.1),
        "nat_rpb": nrm(ks[13], (DEPTH, N_HEADS_B, RPB_H, RPB_W), 0.1),
        "w_branch_a": nrm(ks[14], (DEPTH, D_A, D_MODEL), D_A ** -0.5),
        "w_branch_b": nrm(ks[15], (DEPTH, D_B, D_MODEL), D_B ** -0.5),
        "w_out": nrm(ks[16], (DEPTH, D_MODEL, D_MODEL), D_MODEL ** -0.5),
        "ffn2_norm": gain(ks[17], (DEPTH, D_MODEL)),
        "ffn2_w_gate": nrm(ks[18], (DEPTH, D_MODEL, D_FF), D_MODEL ** -0.5),
        "ffn2_w_up": nrm(ks[19], (DEPTH, D_MODEL, D_FF), D_MODEL ** -0.5),
        "ffn2_w_down": nrm(ks[20], (DEPTH, D_FF, D_MODEL), D_FF ** -0.5),
        "final_norm": gain(ks[21], (D_MODEL,)),
    }


def reference(x_prompt, x_sample, ffn1_norm, ffn1_w_gate, ffn1_w_up, ffn1_w_down, mix_norm, w_in,
              b_gate, sgu_ln_g, sgu_ln_b, sgu_w_s, sgu_b_s, nat_rpb, w_branch_a, w_branch_b, w_out,
              ffn2_norm, ffn2_w_gate, ffn2_w_up, ffn2_w_down, final_norm):
    y_prompt = trunk(x_prompt, ffn1_norm, ffn1_w_gate, ffn1_w_up, ffn1_w_down, mix_norm, w_in, b_gate,
                     sgu_ln_g, sgu_ln_b, sgu_w_s, sgu_b_s, nat_rpb, w_branch_a, w_branch_b, w_out,
                     ffn2_norm, ffn2_w_gate, ffn2_w_up, ffn2_w_down, final_norm)
    y_sample = trunk(x_sample, ffn1_norm, ffn1_w_gate, ffn1_w_up, ffn1_w_down, mix_norm, w_in, b_gate,
                     sgu_ln_g, sgu_ln_b, sgu_w_s, sgu_b_s, nat_rpb, w_branch_a, w_branch_b, w_out,
                     ffn2_norm, ffn2_w_gate, ffn2_w_up, ffn2_w_down, final_norm)
    return (y_prompt, y_sample)
```

```python
import functools

import jax
import jax.numpy as jnp
from jax import lax
from jax.experimental import pallas as pl
from jax.experimental.pallas import tpu as pltpu

D_MODEL = 1024
GRID_W = 64
CHUNK = 128
D_A = 1024
N_GROUPS_A = 8
GROUP_A = D_A // N_GROUPS_A
N_HEADS_B = 16
HEAD_DIM_B = 64
D_B = N_HEADS_B * HEAD_DIM_B
WIN_H = 8
WIN_W = 16
D_FF = 2816
EPS = 1e-6

LANES = 128
HEADS_PER_BLOCK = LANES // HEAD_DIM_B
N_HEAD_BLOCKS = N_HEADS_B // HEADS_PER_BLOCK
BAND = WIN_H * GRID_W
FF_CHUNK = 1408
TOKEN_TILE = 512
VMEM_LIMIT = 56 * 1024 * 1024
MASKED = -1e30

F32 = jnp.float32
BF16 = jnp.bfloat16


def _dot(a, b):
    return jnp.dot(a, b, preferred_element_type=F32)


def _rmsnorm(x, g):
    return x * lax.rsqrt(jnp.mean(x * x, axis=-1, keepdims=True) + EPS) * g


def _gelu(x):
    return 0.5 * x * (1.0 + lax.erf(x * (0.5 ** 0.5)))


def _swiglu_half_step(x, g_ref, wg_ref, wu_ref, wd_ref):
    n = _rmsnorm(x, g_ref[...]).astype(BF16)
    acc = None
    for c in range(D_FF // FF_CHUNK):
        sl = slice(c * FF_CHUNK, (c + 1) * FF_CHUNK)
        gate = _dot(n, wg_ref[:, sl])
        up = _dot(n, wu_ref[:, sl])
        act = (jax.nn.silu(gate) * up).astype(BF16)
        part = _dot(act, wd_ref[sl, :])
        acc = part if acc is None else acc + part
    return x + 0.5 * acc


def _ffn_kernel(h_ref, g_ref, wg_ref, wu_ref, wd_ref, o_ref):
    o_ref[...] = _swiglu_half_step(h_ref[...], g_ref, wg_ref, wu_ref, wd_ref)


def _proj_kernel(h_ref, gn_ref, win_ref, bg_ref, lng_ref, lnb_ref, ws_ref, bs_ref, wba_ref,
                 ma_ref, gb_ref, q_ref, k_ref, v_ref, ya_ref):
    tm = h_ref.shape[0]
    n = _rmsnorm(h_ref[...], gn_ref[...]).astype(BF16)

    def proj(i, width):
        return _dot(n, win_ref[:, i:i + width])

    o_u, o_v, o_q, o_k, o_vv = 0, D_A, 2 * D_A, 2 * D_A + D_B, 2 * D_A + 2 * D_B
    o_ga, o_gb = 2 * D_A + 3 * D_B, 2 * D_A + 3 * D_B + D_MODEL

    q_ref[...] = (proj(o_q, D_B) * (HEAD_DIM_B ** -0.5)).astype(BF16)
    k_ref[...] = proj(o_k, D_B).astype(BF16)
    v_ref[...] = proj(o_vv, D_B).astype(BF16)
    gb_ref[...] = jax.nn.sigmoid(proj(o_gb, D_MODEL) + bg_ref[1:2, :]).astype(BF16)

    vg = _gelu(proj(o_v, D_A))
    mu = jnp.mean(vg, axis=-1, keepdims=True)
    vc = vg - mu
    var = jnp.mean(vc * vc, axis=-1, keepdims=True)
    vn = (vc * lax.rsqrt(var + EPS) * lng_ref[...] + lnb_ref[...]).astype(BF16)
    u = _gelu(proj(o_u, D_A))
    for c in range(tm // CHUNK):
        rows = slice(c * CHUNK, (c + 1) * CHUNK)
        for g in range(N_GROUPS_A):
            cols = slice(g * GROUP_A, (g + 1) * GROUP_A)
            mixed = _dot(ws_ref[g], vn[rows, cols]) + bs_ref[:, cols]
            ya_ref[rows, cols] = (u[rows, cols] * mixed).astype(BF16)

    gate_a = jax.nn.sigmoid(proj(o_ga, D_MODEL) + bg_ref[0:1, :])
    ma_ref[...] = (gate_a * _dot(ya_ref[...], wba_ref[...])).astype(BF16)


def _nat_kernel(q_ref, k_ref, v_ref, bias_ref, o_ref, *, rows):
    lane = lax.broadcasted_iota(jnp.int32, (GRID_W, LANES), 1)
    first_head = lane < HEAD_DIM_B
    zero = jnp.zeros((GRID_W, LANES), BF16)

    def one_row(r, carry):
        band_start = jnp.clip(r - WIN_H // 2, 0, rows - WIN_H)
        d = r - band_start
        q0 = pl.multiple_of(r * GRID_W, GRID_W)
        k0 = pl.multiple_of(band_start * GRID_W, GRID_W)
        qr = q_ref[pl.ds(q0, GRID_W), :]
        kb = k_ref[pl.ds(k0, BAND), :]
        vb = v_ref[pl.ds(k0, BAND), :]
        outs = []
        for h in range(HEADS_PER_BLOCK):
            qh = jnp.where(first_head if h == 0 else jnp.logical_not(first_head), qr, zero)
            s = lax.dot_general(qh, kb, (((1,), (1,)), ((), ())), preferred_element_type=F32)
            s = s + bias_ref[h, d]
            m = jnp.max(s, axis=-1, keepdims=True)
            p = jnp.exp(s - m)
            denom = jnp.sum(p, axis=-1, keepdims=True)
            outs.append(_dot(p.astype(BF16), vb) / denom)
        o_ref[pl.ds(q0, GRID_W), :] = jnp.where(first_head, outs[0], outs[1]).astype(BF16)
        return carry

    lax.fori_loop(0, rows, one_row, 0)


def _merge_kernel(h_ref, ma_ref, gb_ref, yb_ref, wbb_ref, wo_ref, g_ref, wg_ref, wu_ref, wd_ref,
                  *rest, final):
    o_ref = rest[-1]
    merged = ma_ref[...].astype(F32) + gb_ref[...].astype(F32) * _dot(yb_ref[...], wbb_ref[...])
    h2 = h_ref[...] + _dot(merged.astype(BF16), wo_ref[...])
    h3 = _swiglu_half_step(h2, g_ref, wg_ref, wu_ref, wd_ref)
    if final:
        h3 = _rmsnorm(h3, rest[0][...])
    o_ref[...] = h3


def _resident(shape):
    nd = len(shape)
    return pl.BlockSpec(shape, lambda *_: (0,) * nd, pipeline_mode=pl.Buffered(1))


def _token_spec(tm, width):
    return pl.BlockSpec((tm, width), lambda i: (i, 0))


def _params(n_axes=1):
    return pltpu.CompilerParams(dimension_semantics=("arbitrary",) * n_axes,
                                vmem_limit_bytes=VMEM_LIMIT)


def _token_tile(t):
    tm = min(TOKEN_TILE, t)
    assert t % tm == 0 and tm % CHUNK == 0
    return tm


def _ffn_call(h, g, wg, wu, wd):
    t = h.shape[0]
    tm = _token_tile(t)
    return pl.pallas_call(
        _ffn_kernel,
        grid=(t // tm,),
        in_specs=[_token_spec(tm, D_MODEL), _resident(g.shape), _resident(wg.shape),
                  _resident(wu.shape), _resident(wd.shape)],
        out_specs=_token_spec(tm, D_MODEL),
        out_shape=jax.ShapeDtypeStruct((t, D_MODEL), F32),
        compiler_params=_params(),
        name="ffn",
    )(h, g, wg, wu, wd)


def _proj_call(h, gn, win, bg, lng, lnb, ws, bs, wba):
    t = h.shape[0]
    tm = _token_tile(t)
    consts = (gn, win, bg, lng, lnb, ws, bs, wba)
    act = jax.ShapeDtypeStruct((t, D_MODEL), BF16)
    return pl.pallas_call(
        _proj_kernel,
        grid=(t // tm,),
        in_specs=[_token_spec(tm, D_MODEL)] + [_resident(c.shape) for c in consts],
        out_specs=[_token_spec(tm, D_MODEL)] * 5,
        out_shape=[act] * 5,
        scratch_shapes=[pltpu.VMEM((tm, D_A), BF16)],
        compiler_params=_params(),
        name="proj",
    )(h, *consts)


def _nat_call(q, k, v, bias):
    bsz, s, _ = q.shape
    rows = s // GRID_W
    assert s % GRID_W == 0 and rows >= WIN_H
    seq_spec = pl.BlockSpec((None, s, LANES), lambda hb, b: (b, 0, hb))
    bias_spec = pl.BlockSpec((None,) + bias.shape[1:], lambda hb, b: (hb, 0, 0, 0, 0))
    return pl.pallas_call(
        functools.partial(_nat_kernel, rows=rows),
        grid=(N_HEAD_BLOCKS, bsz),
        in_specs=[seq_spec, seq_spec, seq_spec, bias_spec],
        out_specs=seq_spec,
        out_shape=jax.ShapeDtypeStruct((bsz, s, D_B), BF16),
        compiler_params=_params(2),
        name="nat",
    )(q, k, v, bias)


def _merge_call(h, ma, gb, yb, wbb, wo, g, wg, wu, wd, final_norm):
    t = h.shape[0]
    tm = _token_tile(t)
    consts = (wbb, wo, g, wg, wu, wd) + (() if final_norm is None else (final_norm,))
    return pl.pallas_call(
        functools.partial(_merge_kernel, final=final_norm is not None),
        grid=(t // tm,),
        in_specs=[_token_spec(tm, D_MODEL)] * 4 + [_resident(c.shape) for c in consts],
        out_specs=_token_spec(tm, D_MODEL),
        out_shape=jax.ShapeDtypeStruct((t, D_MODEL), F32),
        compiler_params=_params(),
        name="merge",
    )(h, ma, gb, yb, *consts)


def _dense_bias(rpb):
    c = jnp.arange(GRID_W, dtype=jnp.int32)
    col_start = jnp.clip(c - WIN_W // 2, 0, GRID_W - WIN_W)
    kc = jnp.arange(GRID_W, dtype=jnp.int32)
    valid = (kc[None, :] >= col_start[:, None]) & (kc[None, :] < col_start[:, None] + WIN_W)
    col_off = jnp.clip(kc[None, :] - c[:, None] + (WIN_W - 1), 0, 2 * WIN_W - 2)
    d = jnp.arange(WIN_H, dtype=jnp.int32)
    j = jnp.arange(WIN_H, dtype=jnp.int32)
    row_off = j[None, :] + (WIN_H - 1) - d[:, None]
    tab = rpb[:, row_off[:, :, None, None], col_off[None, None, :, :]]
    tab = jnp.where(valid[None, None, None], tab, MASKED)
    tab = jnp.transpose(tab, (0, 1, 3, 2, 4)).reshape(N_HEADS_B, WIN_H, GRID_W, BAND)
    return tab.reshape(N_HEAD_BLOCKS, HEADS_PER_BLOCK, WIN_H, GRID_W, BAND)


def _trunk(x, p, depth):
    bsz, s, _ = x.shape
    t = bsz * s
    h = x.reshape(t, D_MODEL)
    for l in range(depth):
        h = _ffn_call(h, p["ffn1_norm"][l], p["ffn1_w_gate"][l], p["ffn1_w_up"][l], p["ffn1_w_down"][l])
        ma, gb, q, k, v = _proj_call(h, p["mix_norm"][l], p["w_in"][l], p["b_gate"][l], p["sgu_ln_g"][l],
                                     p["sgu_ln_b"][l], p["sgu_w_s"][l], p["sgu_b_s"][l], p["w_branch_a"][l])
        shape3 = (bsz, s, D_B)
        yb = _nat_call(q.reshape(shape3), k.reshape(shape3), v.reshape(shape3), p["nat_bias"][l])
        h = _merge_call(h, ma, gb, yb.reshape(t, D_B), p["w_branch_b"][l], p["w_out"][l],
                        p["ffn2_norm"][l], p["ffn2_w_gate"][l], p["ffn2_w_up"][l], p["ffn2_w_down"][l],
                        p["final_norm"] if l == depth - 1 else None)
    return h.reshape(bsz, s, D_MODEL)


def kernel(x_prompt, x_sample, ffn1_norm, ffn1_w_gate, ffn1_w_up, ffn1_w_down, mix_norm, w_in, b_gate, sgu_ln_g, sgu_ln_b, sgu_w_s, sgu_b_s, nat_rpb, w_branch_a, w_branch_b, w_out, ffn2_norm, ffn2_w_gate, ffn2_w_up, ffn2_w_down, final_norm):
    depth = w_in.shape[0]
    row = lambda a: a.reshape(depth, 1, a.shape[-1])
    bs = jnp.repeat(jnp.transpose(sgu_b_s, (0, 2, 1)), GROUP_A, axis=-1)
    p = dict(
        ffn1_norm=row(ffn1_norm), ffn1_w_gate=ffn1_w_gate.astype(BF16), ffn1_w_up=ffn1_w_up.astype(BF16),
        ffn1_w_down=ffn1_w_down.astype(BF16), mix_norm=row(mix_norm), w_in=w_in.astype(BF16), b_gate=b_gate,
        sgu_ln_g=row(sgu_ln_g), sgu_ln_b=row(sgu_ln_b), sgu_w_s=sgu_w_s.astype(BF16), sgu_b_s=bs,
        nat_bias=jax.vmap(_dense_bias)(nat_rpb), w_branch_a=w_branch_a.astype(BF16),
        w_branch_b=w_branch_b.astype(BF16), w_out=w_out.astype(BF16), ffn2_norm=row(ffn2_norm),
        ffn2_w_gate=ffn2_w_gate.astype(BF16), ffn2_w_up=ffn2_w_up.astype(BF16),
        ffn2_w_down=ffn2_w_down.astype(BF16), final_norm=final_norm.reshape(1, D_MODEL),
    )
    return (_trunk(x_prompt, p, depth), _trunk(x_sample, p, depth))
```

```python
import functools

import jax
import jax.numpy as jnp
from jax import lax
from jax.experimental import pallas as pl
from jax.experimental.pallas import tpu as pltpu

D_MODEL = 1024
GRID_W = 64
CHUNK = 128
D_A = 1024
N_GROUPS_A = 8
GROUP_A = D_A // N_GROUPS_A
N_HEADS_B = 16
HEAD_DIM_B = 64
D_B = N_HEADS_B * HEAD_DIM_B
WIN_H = 8
WIN_W = 16
D_FF = 2816
EPS = 1e-6

LANES = 128
HEADS_PER_BLOCK = LANES // HEAD_DIM_B
N_HEAD_BLOCKS = N_HEADS_B // HEADS_PER_BLOCK
BAND = WIN_H * GRID_W
NAT_ROWS_PER_STEP = 16
NAT_LOOKAHEAD = 4
NAT_LEAD = 2
FF_CHUNK = 1408
TOKEN_TILE = 512
VMEM_LIMIT = 56 * 1024 * 1024
MASKED = -1e30

F32 = jnp.float32
BF16 = jnp.bfloat16


def _dot(a, b):
    return jnp.dot(a, b, preferred_element_type=F32)


def _rmsnorm(x, g):
    return x * lax.rsqrt(jnp.mean(x * x, axis=-1, keepdims=True) + EPS) * g


def _gelu(x):
    return 0.5 * x * (1.0 + lax.erf(x * (0.5 ** 0.5)))


def _swiglu_half_step(x, g_ref, wg_ref, wu_ref, wd_ref):
    n = _rmsnorm(x, g_ref[...]).astype(BF16)
    acc = None
    for c in range(D_FF // FF_CHUNK):
        sl = slice(c * FF_CHUNK, (c + 1) * FF_CHUNK)
        gate = _dot(n, wg_ref[:, sl])
        up = _dot(n, wu_ref[:, sl])
        act = (jax.nn.silu(gate) * up).astype(BF16)
        part = _dot(act, wd_ref[sl, :])
        acc = part if acc is None else acc + part
    return x + 0.5 * acc


def _ffn_kernel(h_ref, g_ref, wg_ref, wu_ref, wd_ref, o_ref):
    o_ref[...] = _swiglu_half_step(h_ref[...], g_ref, wg_ref, wu_ref, wd_ref)


def _proj_kernel(h_ref, gn_ref, win_ref, bg_ref, lng_ref, lnb_ref, ws_ref, bs_ref, wba_ref,
                 ma_ref, gb_ref, q_ref, k_ref, v_ref, ya_ref):
    tm = h_ref.shape[0]
    n = _rmsnorm(h_ref[...], gn_ref[...]).astype(BF16)

    def proj(i, width):
        return _dot(n, win_ref[:, i:i + width])

    o_u, o_v, o_q, o_k, o_vv = 0, D_A, 2 * D_A, 2 * D_A + D_B, 2 * D_A + 2 * D_B
    o_ga, o_gb = 2 * D_A + 3 * D_B, 2 * D_A + 3 * D_B + D_MODEL

    q_ref[...] = (proj(o_q, D_B) * (HEAD_DIM_B ** -0.5)).astype(BF16)
    k_ref[...] = proj(o_k, D_B).astype(BF16)
    v_ref[...] = proj(o_vv, D_B).astype(BF16)
    gb_ref[...] = jax.nn.sigmoid(proj(o_gb, D_MODEL) + bg_ref[1:2, :]).astype(BF16)

    vg = _gelu(proj(o_v, D_A))
    mu = jnp.mean(vg, axis=-1, keepdims=True)
    vc = vg - mu
    var = jnp.mean(vc * vc, axis=-1, keepdims=True)
    vn = (vc * lax.rsqrt(var + EPS) * lng_ref[...] + lnb_ref[...]).astype(BF16)
    u = _gelu(proj(o_u, D_A))
    for c in range(tm // CHUNK):
        rows = slice(c * CHUNK, (c + 1) * CHUNK)
        for g in range(N_GROUPS_A):
            cols = slice(g * GROUP_A, (g + 1) * GROUP_A)
            mixed = _dot(ws_ref[g], vn[rows, cols]) + bs_ref[:, cols]
            ya_ref[rows, cols] = (u[rows, cols] * mixed).astype(BF16)

    gate_a = jax.nn.sigmoid(proj(o_ga, D_MODEL) + bg_ref[0:1, :])
    ma_ref[...] = (gate_a * _dot(ya_ref[...], wba_ref[...])).astype(BF16)


def _nat_kernel(q_ref, k_ref, v_ref, bias_ref, o_ref, p_ring, r_ring, *, rows):
    lane = lax.broadcasted_iota(jnp.int32, (GRID_W, LANES), 1)
    first_head = lane < HEAD_DIM_B
    zero = jnp.zeros((GRID_W, LANES), BF16)

    def band_of(r):
        band_start = jnp.clip(r - WIN_H // 2, 0, rows - WIN_H)
        return r - band_start, pl.multiple_of(band_start * GRID_W, GRID_W)

    def scores(r, slot):
        r = jnp.minimum(r, rows - 1)
        d, k0 = band_of(r)
        qr = q_ref[pl.ds(pl.multiple_of(r * GRID_W, GRID_W), GRID_W), :]
        kb = k_ref[pl.ds(k0, BAND), :]
        qs = jnp.concatenate([jnp.where(first_head, qr, zero), jnp.where(first_head, zero, qr)], axis=0)
        s = lax.dot_general(qs, kb, (((1,), (1,)), ((), ())), preferred_element_type=F32)
        s = s + bias_ref[d]
        p = jnp.exp(s - jnp.max(s, axis=-1, keepdims=True))
        p_ring[slot] = p.astype(BF16)
        r_ring[slot] = jnp.broadcast_to(1.0 / jnp.sum(p, axis=-1, keepdims=True), (HEADS_PER_BLOCK * GRID_W, LANES))

    def output(r, slot):
        _, k0 = band_of(r)
        o = _dot(p_ring[slot], v_ref[pl.ds(k0, BAND), :]) * r_ring[slot]
        q0 = pl.multiple_of(r * GRID_W, GRID_W)
        o_ref[pl.ds(q0, GRID_W), :] = jnp.where(first_head, o[:GRID_W], o[GRID_W:]).astype(BF16)

    for r in range(NAT_LOOKAHEAD):
        scores(r, r)

    def row_group(g, carry):
        base = g * NAT_ROWS_PER_STEP
        ahead = NAT_LOOKAHEAD + NAT_LEAD
        for i in range(NAT_LOOKAHEAD, ahead):
            scores(base + i, i % NAT_ROWS_PER_STEP)
        for i in range(NAT_ROWS_PER_STEP):
            output(base + i, i)
            if i + NAT_LEAD < NAT_ROWS_PER_STEP:
                scores(base + ahead + i, (ahead + i) % NAT_ROWS_PER_STEP)
        return carry

    lax.fori_loop(0, rows // NAT_ROWS_PER_STEP, row_group, 0)


def _merge_kernel(h_ref, ma_ref, gb_ref, yb_ref, wbb_ref, wo_ref, g_ref, wg_ref, wu_ref, wd_ref,
                  *rest, final):
    o_ref = rest[-1]
    merged = ma_ref[...].astype(F32) + gb_ref[...].astype(F32) * _dot(yb_ref[...], wbb_ref[...])
    h2 = h_ref[...] + _dot(merged.astype(BF16), wo_ref[...])
    h3 = _swiglu_half_step(h2, g_ref, wg_ref, wu_ref, wd_ref)
    if final:
        h3 = _rmsnorm(h3, rest[0][...])
    o_ref[...] = h3


def _resident(shape):
    nd = len(shape)
    return pl.BlockSpec(shape, lambda *_: (0,) * nd, pipeline_mode=pl.Buffered(1))


def _token_spec(tm, width):
    return pl.BlockSpec((tm, width), lambda i: (i, 0))


def _params(n_axes=1):
    return pltpu.CompilerParams(dimension_semantics=("arbitrary",) * n_axes,
                                vmem_limit_bytes=VMEM_LIMIT)


def _token_tile(t):
    tm = min(TOKEN_TILE, t)
    assert t % tm == 0 and tm % CHUNK == 0
    return tm


def _ffn_call(h, g, wg, wu, wd):
    t = h.shape[0]
    tm = _token_tile(t)
    return pl.pallas_call(
        _ffn_kernel,
        grid=(t // tm,),
        in_specs=[_token_spec(tm, D_MODEL), _resident(g.shape), _resident(wg.shape),
                  _resident(wu.shape), _resident(wd.shape)],
        out_specs=_token_spec(tm, D_MODEL),
        out_shape=jax.ShapeDtypeStruct((t, D_MODEL), F32),
        compiler_params=_params(),
        name="ffn",
    )(h, g, wg, wu, wd)


def _proj_call(h, gn, win, bg, lng, lnb, ws, bs, wba):
    t = h.shape[0]
    tm = _token_tile(t)
    consts = (gn, win, bg, lng, lnb, ws, bs, wba)
    act = jax.ShapeDtypeStruct((t, D_MODEL), BF16)
    return pl.pallas_call(
        _proj_kernel,
        grid=(t // tm,),
        in_specs=[_token_spec(tm, D_MODEL)] + [_resident(c.shape) for c in consts],
        out_specs=[_token_spec(tm, D_MODEL)] * 5,
        out_shape=[act] * 5,
        scratch_shapes=[pltpu.VMEM((tm, D_A), BF16)],
        compiler_params=_params(),
        name="proj",
    )(h, *consts)


def _nat_call(q, k, v, bias):
    bsz, s, _ = q.shape
    rows = s // GRID_W
    assert s % GRID_W == 0 and rows >= WIN_H and rows % NAT_ROWS_PER_STEP == 0
    seq_spec = pl.BlockSpec((None, s, LANES), lambda hb, b: (b, 0, hb))
    bias_spec = pl.BlockSpec((None,) + bias.shape[1:], lambda hb, b: (hb, 0, 0, 0))
    return pl.pallas_call(
        functools.partial(_nat_kernel, rows=rows),
        grid=(N_HEAD_BLOCKS, bsz),
        in_specs=[seq_spec, seq_spec, seq_spec, bias_spec],
        out_specs=seq_spec,
        out_shape=jax.ShapeDtypeStruct((bsz, s, D_B), BF16),
        scratch_shapes=[pltpu.VMEM((NAT_ROWS_PER_STEP, HEADS_PER_BLOCK * GRID_W, BAND), BF16),
                        pltpu.VMEM((NAT_ROWS_PER_STEP, HEADS_PER_BLOCK * GRID_W, LANES), F32)],
        compiler_params=_params(2),
        name="nat",
    )(q, k, v, bias)


def _merge_call(h, ma, gb, yb, wbb, wo, g, wg, wu, wd, final_norm):
    t = h.shape[0]
    tm = _token_tile(t)
    consts = (wbb, wo, g, wg, wu, wd) + (() if final_norm is None else (final_norm,))
    return pl.pallas_call(
        functools.partial(_merge_kernel, final=final_norm is not None),
        grid=(t // tm,),
        in_specs=[_token_spec(tm, D_MODEL)] * 4 + [_resident(c.shape) for c in consts],
        out_specs=_token_spec(tm, D_MODEL),
        out_shape=jax.ShapeDtypeStruct((t, D_MODEL), F32),
        compiler_params=_params(),
        name="merge",
    )(h, ma, gb, yb, *consts)


def _dense_bias(rpb):
    pad = GRID_W - WIN_W
    padded = jnp.pad(rpb, ((0, 0), (0, 0), (pad, pad)), constant_values=MASKED)
    by_col = jnp.stack([padded[:, :, GRID_W - 1 - c:2 * GRID_W - 1 - c] for c in range(GRID_W)], axis=2)
    c = jnp.arange(GRID_W, dtype=jnp.int32)
    col_start = jnp.clip(c - WIN_W // 2, 0, GRID_W - WIN_W)
    kc = jnp.arange(GRID_W, dtype=jnp.int32)
    valid = (kc[None, :] >= col_start[:, None]) & (kc[None, :] < col_start[:, None] + WIN_W)
    by_col = jnp.where(valid[None, None], by_col, MASKED)
    tab = jnp.stack([by_col[:, WIN_H - 1 - d:2 * WIN_H - 1 - d] for d in range(WIN_H)], axis=1)
    tab = jnp.transpose(tab, (0, 1, 3, 2, 4)).reshape(N_HEAD_BLOCKS, HEADS_PER_BLOCK, WIN_H, GRID_W, BAND)
    return jnp.transpose(tab, (0, 2, 1, 3, 4)).reshape(N_HEAD_BLOCKS, WIN_H, HEADS_PER_BLOCK * GRID_W, BAND)


def _trunk(x, p, depth):
    bsz, s, _ = x.shape
    t = bsz * s
    h = x.reshape(t, D_MODEL)
    for l in range(depth):
        h = _ffn_call(h, p["ffn1_norm"][l], p["ffn1_w_gate"][l], p["ffn1_w_up"][l], p["ffn1_w_down"][l])
        ma, gb, q, k, v = _proj_call(h, p["mix_norm"][l], p["w_in"][l], p["b_gate"][l], p["sgu_ln_g"][l],
                                     p["sgu_ln_b"][l], p["sgu_w_s"][l], p["sgu_b_s"][l], p["w_branch_a"][l])
        shape3 = (bsz, s, D_B)
        yb = _nat_call(q.reshape(shape3), k.reshape(shape3), v.reshape(shape3), p["nat_bias"][l])
        h = _merge_call(h, ma, gb, yb.reshape(t, D_B), p["w_branch_b"][l], p["w_out"][l],
                        p["ffn2_norm"][l], p["ffn2_w_gate"][l], p["ffn2_w_up"][l], p["ffn2_w_down"][l],
                        p["final_norm"] if l == depth - 1 else None)
    return h.reshape(bsz, s, D_MODEL)


def kernel(x_prompt, x_sample, ffn1_norm, ffn1_w_gate, ffn1_w_up, ffn1_w_down, mix_norm, w_in, b_gate, sgu_ln_g, sgu_ln_b, sgu_w_s, sgu_b_s, nat_rpb, w_branch_a, w_branch_b, w_out, ffn2_norm, ffn2_w_gate, ffn2_w_up, ffn2_w_down, final_norm):
    depth = w_in.shape[0]
    row = lambda a: a.reshape(depth, 1, a.shape[-1])
    bs = jnp.repeat(jnp.transpose(sgu_b_s, (0, 2, 1)), GROUP_A, axis=-1)
    p = dict(
        ffn1_norm=row(ffn1_norm), ffn1_w_gate=ffn1_w_gate.astype(BF16), ffn1_w_up=ffn1_w_up.astype(BF16),
        ffn1_w_down=ffn1_w_down.astype(BF16), mix_norm=row(mix_norm), w_in=w_in.astype(BF16), b_gate=b_gate,
        sgu_ln_g=row(sgu_ln_g), sgu_ln_b=row(sgu_ln_b), sgu_w_s=sgu_w_s.astype(BF16), sgu_b_s=bs,
        nat_bias=jax.vmap(_dense_bias)(nat_rpb), w_branch_a=w_branch_a.astype(BF16),
        w_branch_b=w_branch_b.astype(BF16), w_out=w_out.astype(BF16), ffn2_norm=row(ffn2_norm),
        ffn2_w_gate=ffn2_w_gate.astype(BF16), ffn2_w_up=ffn2_w_up.astype(BF16),
        ffn2_w_down=ffn2_w_down.astype(BF16), final_norm=final_norm.reshape(1, D_MODEL),
    )
    return (_trunk(x_prompt, p, depth), _trunk(x_sample, p, depth))
```

```python
import functools

import jax
import jax.numpy as jnp
from jax import lax
from jax.experimental import pallas as pl
from jax.experimental.pallas import tpu as pltpu

D_MODEL = 1024
GRID_W = 64
CHUNK = 128
D_A = 1024
N_GROUPS_A = 8
GROUP_A = D_A // N_GROUPS_A
N_HEADS_B = 16
HEAD_DIM_B = 64
D_B = N_HEADS_B * HEAD_DIM_B
WIN_H = 8
WIN_W = 16
D_FF = 2816
EPS = 1e-6

LANES = 128
HEADS_PER_BLOCK = LANES // HEAD_DIM_B
N_HEAD_BLOCKS = N_HEADS_B // HEADS_PER_BLOCK
BAND = WIN_H * GRID_W
NAT_ROWS_PER_STEP = 16
NAT_LOOKAHEAD = 4
NAT_LEAD = 5
MXU_WIDTH = 256
_FF_SPLIT = MXU_WIDTH * ((D_FF // MXU_WIDTH + 1) // 2)
FF_RANGES = ((0, _FF_SPLIT), (_FF_SPLIT, D_FF))
TOKEN_TILE = 512
VMEM_LIMIT = 56 * 1024 * 1024
MASKED = -1e30

F32 = jnp.float32
BF16 = jnp.bfloat16


def _dot(a, b):
    return jnp.dot(a, b, preferred_element_type=F32)


def _rmsnorm(x, g):
    return x * lax.rsqrt(jnp.mean(x * x, axis=-1, keepdims=True) + EPS) * g


def _gelu(x):
    return 0.5 * x * (1.0 + lax.erf(x * (0.5 ** 0.5)))


def _swiglu_half_step(x, g_ref, wg_ref, wu_ref, wd_ref):
    n = _rmsnorm(x, g_ref[...]).astype(BF16)
    acc = None
    for lo, hi in FF_RANGES:
        sl = slice(lo, hi)
        gate = _dot(n, wg_ref[:, sl])
        up = _dot(n, wu_ref[:, sl])
        act = (jax.nn.silu(gate) * up).astype(BF16)
        part = _dot(act, wd_ref[sl, :])
        acc = part if acc is None else acc + part
    return x + 0.5 * acc


def _ffn_kernel(h_ref, g_ref, wg_ref, wu_ref, wd_ref, o_ref):
    o_ref[...] = _swiglu_half_step(h_ref[...], g_ref, wg_ref, wu_ref, wd_ref)


def _proj_kernel(h_ref, gn_ref, win_ref, bg_ref, lng_ref, lnb_ref, ws_ref, bs_ref, wba_ref,
                 ma_ref, gb_ref, q_ref, k_ref, v_ref, ya_ref):
    tm = h_ref.shape[0]
    n = _rmsnorm(h_ref[...], gn_ref[...]).astype(BF16)

    def proj(i, width):
        return _dot(n, win_ref[:, i:i + width])

    o_u, o_v, o_q, o_k, o_vv = 0, D_A, 2 * D_A, 2 * D_A + D_B, 2 * D_A + 2 * D_B
    o_ga, o_gb = 2 * D_A + 3 * D_B, 2 * D_A + 3 * D_B + D_MODEL

    q_ref[...] = (proj(o_q, D_B) * (HEAD_DIM_B ** -0.5)).astype(BF16)
    k_ref[...] = proj(o_k, D_B).astype(BF16)
    v_ref[...] = proj(o_vv, D_B).astype(BF16)
    gb_ref[...] = jax.nn.sigmoid(proj(o_gb, D_MODEL) + bg_ref[1:2, :]).astype(BF16)

    vg = _gelu(proj(o_v, D_A))
    mu = jnp.mean(vg, axis=-1, keepdims=True)
    vc = vg - mu
    var = jnp.mean(vc * vc, axis=-1, keepdims=True)
    vn = (vc * lax.rsqrt(var + EPS) * lng_ref[...] + lnb_ref[...]).astype(BF16)
    u = _gelu(proj(o_u, D_A))
    for c in range(tm // CHUNK):
        rows = slice(c * CHUNK, (c + 1) * CHUNK)
        for g in range(N_GROUPS_A):
            cols = slice(g * GROUP_A, (g + 1) * GROUP_A)
            mixed = _dot(ws_ref[g], vn[rows, cols]) + bs_ref[:, cols]
            ya_ref[rows, cols] = (u[rows, cols] * mixed).astype(BF16)

    gate_a = jax.nn.sigmoid(proj(o_ga, D_MODEL) + bg_ref[0:1, :])
    ma_ref[...] = (gate_a * _dot(ya_ref[...], wba_ref[...])).astype(BF16)


def _nat_kernel(q_ref, k_ref, v_ref, bias_ref, o_ref, p_ring, r_ring, *, rows):
    lane = lax.broadcasted_iota(jnp.int32, (GRID_W, LANES), 1)
    first_head = lane < HEAD_DIM_B
    zero = jnp.zeros((GRID_W, LANES), BF16)

    def band_of(r):
        band_start = jnp.clip(r - WIN_H // 2, 0, rows - WIN_H)
        return r - band_start, pl.multiple_of(band_start * GRID_W, GRID_W)

    def scores(r, slot):
        r = jnp.minimum(r, rows - 1)
        d, k0 = band_of(r)
        qr = q_ref[pl.ds(pl.multiple_of(r * GRID_W, GRID_W), GRID_W), :]
        kb = k_ref[pl.ds(k0, BAND), :]
        qs = jnp.concatenate([jnp.where(first_head, qr, zero), jnp.where(first_head, zero, qr)], axis=0)
        s = lax.dot_general(qs, kb, (((1,), (1,)), ((), ())), preferred_element_type=F32)
        s = s + bias_ref[d]
        p = jnp.exp(s - jnp.max(s, axis=-1, keepdims=True))
        p_ring[slot] = p.astype(BF16)
        r_ring[slot] = jnp.broadcast_to(1.0 / jnp.sum(p, axis=-1, keepdims=True), (HEADS_PER_BLOCK * GRID_W, LANES))

    def output(r, slot):
        _, k0 = band_of(r)
        o = _dot(p_ring[slot], v_ref[pl.ds(k0, BAND), :]) * r_ring[slot]
        q0 = pl.multiple_of(r * GRID_W, GRID_W)
        o_ref[pl.ds(q0, GRID_W), :] = jnp.where(first_head, o[:GRID_W], o[GRID_W:]).astype(BF16)

    for r in range(NAT_LOOKAHEAD):
        scores(r, r)

    def row_group(g, carry):
        base = g * NAT_ROWS_PER_STEP
        ahead = NAT_LOOKAHEAD + NAT_LEAD
        for i in range(NAT_LOOKAHEAD, ahead):
            scores(base + i, i % NAT_ROWS_PER_STEP)
        for i in range(NAT_ROWS_PER_STEP):
            output(base + i, i)
            if i + NAT_LEAD < NAT_ROWS_PER_STEP:
                scores(base + ahead + i, (ahead + i) % NAT_ROWS_PER_STEP)
        return carry

    lax.fori_loop(0, rows // NAT_ROWS_PER_STEP, row_group, 0)


def _merge_kernel(h_ref, ma_ref, gb_ref, yb_ref, wbb_ref, wo_ref, g_ref, wg_ref, wu_ref, wd_ref,
                  *rest, final):
    o_ref = rest[-1]
    merged = ma_ref[...].astype(F32) + gb_ref[...].astype(F32) * _dot(yb_ref[...], wbb_ref[...])
    h2 = h_ref[...] + _dot(merged.astype(BF16), wo_ref[...])
    h3 = _swiglu_half_step(h2, g_ref, wg_ref, wu_ref, wd_ref)
    if final:
        h3 = _rmsnorm(h3, rest[0][...])
    o_ref[...] = h3


def _resident(stacked, layer):
    shape = stacked.shape[1:]
    return pl.BlockSpec((None,) + shape, lambda *_: (layer,) + (0,) * len(shape), pipeline_mode=pl.Buffered(1))


def _token_spec(tm, width):
    return pl.BlockSpec((tm, width), lambda i: (i, 0))


def _params(n_axes=1):
    return pltpu.CompilerParams(dimension_semantics=("arbitrary",) * n_axes,
                                vmem_limit_bytes=VMEM_LIMIT)


def _token_tile(t):
    tm = min(TOKEN_TILE, t)
    assert t % tm == 0 and tm % CHUNK == 0
    return tm


def _ffn_call(layer, h, g, wg, wu, wd):
    t = h.shape[0]
    tm = _token_tile(t)
    return pl.pallas_call(
        _ffn_kernel,
        grid=(t // tm,),
        in_specs=[_token_spec(tm, D_MODEL)] + [_resident(c, layer) for c in (g, wg, wu, wd)],
        out_specs=_token_spec(tm, D_MODEL),
        out_shape=jax.ShapeDtypeStruct((t, D_MODEL), F32),
        compiler_params=_params(),
        name="ffn",
    )(h, g, wg, wu, wd)


def _proj_call(layer, h, gn, win, bg, lng, lnb, ws, bs, wba):
    t = h.shape[0]
    tm = _token_tile(t)
    consts = (gn, win, bg, lng, lnb, ws, bs, wba)
    act = jax.ShapeDtypeStruct((t, D_MODEL), BF16)
    return pl.pallas_call(
        _proj_kernel,
        grid=(t // tm,),
        in_specs=[_token_spec(tm, D_MODEL)] + [_resident(c, layer) for c in consts],
        out_specs=[_token_spec(tm, D_MODEL)] * 5,
        out_shape=[act] * 5,
        scratch_shapes=[pltpu.VMEM((tm, D_A), BF16)],
        compiler_params=_params(),
        name="proj",
    )(h, *consts)


def _nat_call(layer, q, k, v, bias):
    bsz, s, _ = q.shape
    rows = s // GRID_W
    assert s % GRID_W == 0 and rows >= WIN_H and rows % NAT_ROWS_PER_STEP == 0
    seq_spec = pl.BlockSpec((None, s, LANES), lambda hb, b: (b, 0, hb))
    bias_spec = pl.BlockSpec((None, None) + bias.shape[2:], lambda hb, b: (layer, hb, 0, 0, 0))
    return pl.pallas_call(
        functools.partial(_nat_kernel, rows=rows),
        grid=(N_HEAD_BLOCKS, bsz),
        in_specs=[seq_spec, seq_spec, seq_spec, bias_spec],
        out_specs=seq_spec,
        out_shape=jax.ShapeDtypeStruct((bsz, s, D_B), BF16),
        scratch_shapes=[pltpu.VMEM((NAT_ROWS_PER_STEP, HEADS_PER_BLOCK * GRID_W, BAND), BF16),
                        pltpu.VMEM((NAT_ROWS_PER_STEP, HEADS_PER_BLOCK * GRID_W, LANES), F32)],
        compiler_params=_params(2),
        name="nat",
    )(q, k, v, bias)


def _merge_call(layer, h, ma, gb, yb, wbb, wo, g, wg, wu, wd, final_norm):
    t = h.shape[0]
    tm = _token_tile(t)
    consts = (wbb, wo, g, wg, wu, wd)
    const_specs = [_resident(c, layer) for c in consts]
    if final_norm is not None:
        consts += (final_norm,)
        const_specs.append(_resident(final_norm, 0))
    return pl.pallas_call(
        functools.partial(_merge_kernel, final=final_norm is not None),
        grid=(t // tm,),
        in_specs=[_token_spec(tm, D_MODEL)] * 4 + const_specs,
        out_specs=_token_spec(tm, D_MODEL),
        out_shape=jax.ShapeDtypeStruct((t, D_MODEL), F32),
        compiler_params=_params(),
        name="merge",
    )(h, ma, gb, yb, *consts)


def _dense_bias(rpb):
    pad = GRID_W - WIN_W
    padded = jnp.pad(rpb, ((0, 0), (0, 0), (pad, pad)), constant_values=MASKED)
    by_col = jnp.stack([padded[:, :, GRID_W - 1 - c:2 * GRID_W - 1 - c] for c in range(GRID_W)], axis=2)
    c = jnp.arange(GRID_W, dtype=jnp.int32)
    col_start = jnp.clip(c - WIN_W // 2, 0, GRID_W - WIN_W)
    kc = jnp.arange(GRID_W, dtype=jnp.int32)
    valid = (kc[None, :] >= col_start[:, None]) & (kc[None, :] < col_start[:, None] + WIN_W)
    by_col = jnp.where(valid[None, None], by_col, MASKED)
    tab = jnp.stack([by_col[:, WIN_H - 1 - d:2 * WIN_H - 1 - d] for d in range(WIN_H)], axis=1)
    tab = jnp.transpose(tab, (0, 1, 3, 2, 4)).reshape(N_HEAD_BLOCKS, HEADS_PER_BLOCK, WIN_H, GRID_W, BAND)
    return jnp.transpose(tab, (0, 2, 1, 3, 4)).reshape(N_HEAD_BLOCKS, WIN_H, HEADS_PER_BLOCK * GRID_W, BAND)


def _trunk(x, p, depth):
    bsz, s, _ = x.shape
    t = bsz * s
    h = x.reshape(t, D_MODEL)
    for l in range(depth):
        h = _ffn_call(l, h, p["ffn1_norm"], p["ffn1_w_gate"], p["ffn1_w_up"], p["ffn1_w_down"])
        ma, gb, q, k, v = _proj_call(l, h, p["mix_norm"], p["w_in"], p["b_gate"], p["sgu_ln_g"],
                                     p["sgu_ln_b"], p["sgu_w_s"], p["sgu_b_s"], p["w_branch_a"])
        shape3 = (bsz, s, D_B)
        yb = _nat_call(l, q.reshape(shape3), k.reshape(shape3), v.reshape(shape3), p["nat_bias"])
        h = _merge_call(l, h, ma, gb, yb.reshape(t, D_B), p["w_branch_b"], p["w_out"],
                        p["ffn2_norm"], p["ffn2_w_gate"], p["ffn2_w_up"], p["ffn2_w_down"],
                        p["final_norm"] if l == depth - 1 else None)
    return h.reshape(bsz, s, D_MODEL)


def kernel(x_prompt, x_sample, ffn1_norm, ffn1_w_gate, ffn1_w_up, ffn1_w_down, mix_norm, w_in, b_gate, sgu_ln_g, sgu_ln_b, sgu_w_s, sgu_b_s, nat_rpb, w_branch_a, w_branch_b, w_out, ffn2_norm, ffn2_w_gate, ffn2_w_up, ffn2_w_down, final_norm):
    depth = w_in.shape[0]
    row = lambda a: a.reshape(depth, 1, a.shape[-1])
    bs = jnp.repeat(jnp.transpose(sgu_b_s, (0, 2, 1)), GROUP_A, axis=-1)
    p = dict(
        ffn1_norm=row(ffn1_norm), ffn1_w_gate=ffn1_w_gate.astype(BF16), ffn1_w_up=ffn1_w_up.astype(BF16),
        ffn1_w_down=ffn1_w_down.astype(BF16), mix_norm=row(mix_norm), w_in=w_in.astype(BF16), b_gate=b_gate,
        sgu_ln_g=row(sgu_ln_g), sgu_ln_b=row(sgu_ln_b), sgu_w_s=sgu_w_s.astype(BF16), sgu_b_s=bs,
        nat_bias=jax.vmap(_dense_bias)(nat_rpb), w_branch_a=w_branch_a.astype(BF16),
        w_branch_b=w_branch_b.astype(BF16), w_out=w_out.astype(BF16), ffn2_norm=row(ffn2_norm),
        ffn2_w_gate=ffn2_w_gate.astype(BF16), ffn2_w_up=ffn2_w_up.astype(BF16),
        ffn2_w_down=ffn2_w_down.astype(BF16), final_norm=final_norm.reshape(1, 1, D_MODEL),
    )
    return (_trunk(x_prompt, p, depth), _trunk(x_sample, p, depth))
```

```python
import functools

import jax
import jax.numpy as jnp
from jax import lax
from jax.experimental import pallas as pl
from jax.experimental.pallas import tpu as pltpu

D_MODEL = 1024
GRID_W = 64
CHUNK = 128
D_A = 1024
N_GROUPS_A = 8
GROUP_A = D_A // N_GROUPS_A
N_HEADS_B = 16
HEAD_DIM_B = 64
D_B = N_HEADS_B * HEAD_DIM_B
WIN_H = 8
WIN_W = 16
D_FF = 2816
EPS = 1e-6

LANES = 128
HEADS_PER_BLOCK = LANES // HEAD_DIM_B
N_HEAD_BLOCKS = N_HEADS_B // HEADS_PER_BLOCK
BAND = WIN_H * GRID_W
NAT_ROWS_PER_STEP = 16
NAT_ROWS_PER_GRID_STEP = 128
NAT_LOOKAHEAD = 4
NAT_LEAD = 5
MXU_WIDTH = 256
_FF_SPLIT = MXU_WIDTH * ((D_FF // MXU_WIDTH + 1) // 2)
FF_RANGES = ((0, _FF_SPLIT), (_FF_SPLIT, D_FF))
TOKEN_TILE = 512
VMEM_LIMIT = 56 * 1024 * 1024
LOG2E = 1.4426950408889634
MASKED = -1e30

F32 = jnp.float32
BF16 = jnp.bfloat16


def _dot(a, b):
    return jnp.dot(a, b, preferred_element_type=F32)


def _rmsnorm(x, g):
    return x * lax.rsqrt(jnp.mean(x * x, axis=-1, keepdims=True) + EPS) * g


def _gelu(x):
    return 0.5 * x * (1.0 + lax.erf(x * (0.5 ** 0.5)))


def _swiglu_half_step(x, g_ref, wg_ref, wu_ref, wd_ref):
    n = _rmsnorm(x, g_ref[...]).astype(BF16)
    acc = None
    for lo, hi in FF_RANGES:
        sl = slice(lo, hi)
        gate = _dot(n, wg_ref[:, sl])
        up = _dot(n, wu_ref[:, sl])
        act = (jax.nn.silu(gate) * up).astype(BF16)
        part = _dot(act, wd_ref[sl, :])
        acc = part if acc is None else acc + part
    return x + 0.5 * acc


def _ffn_kernel(h_ref, g_ref, wg_ref, wu_ref, wd_ref, o_ref):
    o_ref[...] = _swiglu_half_step(h_ref[...], g_ref, wg_ref, wu_ref, wd_ref)


def _proj_kernel(h_ref, gn_ref, win_ref, bg_ref, lng_ref, lnb_ref, ws_ref, bs_ref, wba_ref,
                 ma_ref, gb_ref, q_ref, k_ref, v_ref, ya_ref):
    tm = h_ref.shape[0]
    n = _rmsnorm(h_ref[...], gn_ref[...]).astype(BF16)

    def proj(i, width):
        return _dot(n, win_ref[:, i:i + width])

    o_u, o_v, o_q, o_k, o_vv = 0, D_A, 2 * D_A, 2 * D_A + D_B, 2 * D_A + 2 * D_B
    o_ga, o_gb = 2 * D_A + 3 * D_B, 2 * D_A + 3 * D_B + D_MODEL

    vg = _gelu(proj(o_v, D_A))
    mu = jnp.mean(vg, axis=-1, keepdims=True)
    vc = vg - mu
    var = jnp.mean(vc * vc, axis=-1, keepdims=True)
    vn = (vc * lax.rsqrt(var + EPS) * lng_ref[...] + lnb_ref[...]).astype(BF16)
    u = _gelu(proj(o_u, D_A))

    q_ref[...] = (proj(o_q, D_B) * (HEAD_DIM_B ** -0.5 * LOG2E)).astype(BF16)
    k_ref[...] = proj(o_k, D_B).astype(BF16)
    v_ref[...] = proj(o_vv, D_B).astype(BF16)
    gb_ref[...] = jax.nn.sigmoid(proj(o_gb, D_MODEL) + bg_ref[1:2, :]).astype(BF16)

    for c in range(tm // CHUNK):
        rows = slice(c * CHUNK, (c + 1) * CHUNK)
        for g in range(N_GROUPS_A):
            cols = slice(g * GROUP_A, (g + 1) * GROUP_A)
            mixed = _dot(ws_ref[g], vn[rows, cols]) + bs_ref[:, cols]
            ya_ref[rows, cols] = (u[rows, cols] * mixed).astype(BF16)

    gate_a = jax.nn.sigmoid(proj(o_ga, D_MODEL) + bg_ref[0:1, :])
    ma_ref[...] = (gate_a * _dot(ya_ref[...], wba_ref[...])).astype(BF16)


def _nat_kernel(q_ref, k_ref, v_ref, bias_ref, o_ref, p_ring, r_ring, *, rows, n_seq):
    lane = lax.broadcasted_iota(jnp.int32, (GRID_W, LANES), 1)
    first_head = lane < HEAD_DIM_B
    zero = jnp.zeros((GRID_W, LANES), BF16)
    total_rows = n_seq * rows

    def band_of(r):
        row_in_seq = lax.rem(r, rows)
        band_start = r - row_in_seq + jnp.clip(row_in_seq - WIN_H // 2, 0, rows - WIN_H)
        return r - band_start, pl.multiple_of(band_start * GRID_W, GRID_W)

    def scores(r, slot):
        r = jnp.minimum(r, total_rows - 1)
        d, k0 = band_of(r)
        qr = q_ref[pl.ds(pl.multiple_of(r * GRID_W, GRID_W), GRID_W), :]
        kb = k_ref[pl.ds(k0, BAND), :]
        qs = jnp.concatenate([jnp.where(first_head, qr, zero), jnp.where(first_head, zero, qr)], axis=0)
        s = lax.dot_general(qs, kb, (((1,), (1,)), ((), ())), preferred_element_type=F32)
        s = s + bias_ref[d]
        p = jnp.exp2(s - jnp.max(s, axis=-1, keepdims=True))
        p_ring[slot] = p.astype(BF16)
        r_ring[slot] = jnp.broadcast_to(1.0 / jnp.sum(p, axis=-1, keepdims=True), (HEADS_PER_BLOCK * GRID_W, LANES))

    def output(r, slot):
        _, k0 = band_of(r)
        o = _dot(p_ring[slot], v_ref[pl.ds(k0, BAND), :]) * r_ring[slot]
        q0 = pl.multiple_of(r * GRID_W, GRID_W)
        o_ref[pl.ds(q0, GRID_W), :] = jnp.where(first_head, o[:GRID_W], o[GRID_W:]).astype(BF16)

    for r in range(NAT_LOOKAHEAD):
        scores(r, r)

    def row_group(g, carry):
        base = g * NAT_ROWS_PER_STEP
        ahead = NAT_LOOKAHEAD + NAT_LEAD
        for i in range(NAT_LOOKAHEAD, ahead):
            scores(base + i, i % NAT_ROWS_PER_STEP)
        for i in range(NAT_ROWS_PER_STEP):
            output(base + i, i)
            if i + NAT_LEAD < NAT_ROWS_PER_STEP:
                scores(base + ahead + i, (ahead + i) % NAT_ROWS_PER_STEP)
        return carry

    lax.fori_loop(0, total_rows // NAT_ROWS_PER_STEP, row_group, 0)


def _merge_kernel(h_ref, ma_ref, gb_ref, yb_ref, wbb_ref, wo_ref, g_ref, wg_ref, wu_ref, wd_ref,
                  *rest, final):
    o_ref = rest[-1]
    merged = ma_ref[...].astype(F32) + gb_ref[...].astype(F32) * _dot(yb_ref[...], wbb_ref[...])
    h2 = h_ref[...] + _dot(merged.astype(BF16), wo_ref[...])
    h3 = _swiglu_half_step(h2, g_ref, wg_ref, wu_ref, wd_ref)
    if final:
        h3 = _rmsnorm(h3, rest[0][...])
    o_ref[...] = h3


def _resident(stacked, layer):
    shape = stacked.shape[1:]
    return pl.BlockSpec((None,) + shape, lambda *_: (layer,) + (0,) * len(shape), pipeline_mode=pl.Buffered(1))


def _token_spec(tm, width):
    return pl.BlockSpec((tm, width), lambda i: (i, 0))


def _params(n_axes=1):
    return pltpu.CompilerParams(dimension_semantics=("arbitrary",) * n_axes,
                                vmem_limit_bytes=VMEM_LIMIT)


def _token_tile(t):
    tm = min(TOKEN_TILE, t)
    assert t % tm == 0 and tm % CHUNK == 0
    return tm


def _ffn_call(layer, h, g, wg, wu, wd):
    t = h.shape[0]
    tm = _token_tile(t)
    return pl.pallas_call(
        _ffn_kernel,
        grid=(t // tm,),
        in_specs=[_token_spec(tm, D_MODEL)] + [_resident(c, layer) for c in (g, wg, wu, wd)],
        out_specs=_token_spec(tm, D_MODEL),
        out_shape=jax.ShapeDtypeStruct((t, D_MODEL), F32),
        compiler_params=_params(),
        name="ffn",
    )(h, g, wg, wu, wd)


def _proj_call(layer, h, gn, win, bg, lng, lnb, ws, bs, wba):
    t = h.shape[0]
    tm = _token_tile(t)
    consts = (gn, win, bg, lng, lnb, ws, bs, wba)
    act = jax.ShapeDtypeStruct((t, D_MODEL), BF16)
    return pl.pallas_call(
        _proj_kernel,
        grid=(t // tm,),
        in_specs=[_token_spec(tm, D_MODEL)] + [_resident(c, layer) for c in consts],
        out_specs=[_token_spec(tm, D_MODEL)] * 5,
        out_shape=[act] * 5,
        scratch_shapes=[pltpu.VMEM((tm, D_A), BF16)],
        compiler_params=_params(),
        name="proj",
    )(h, *consts)


def _nat_call(layer, q, k, v, bias):
    bsz, s, _ = q.shape
    rows = s // GRID_W
    assert s % GRID_W == 0 and rows >= WIN_H and rows % NAT_ROWS_PER_STEP == 0
    n_seq = max(n for n in range(1, bsz + 1) if bsz % n == 0 and n * rows <= max(rows, NAT_ROWS_PER_GRID_STEP))
    grouped = (bsz // n_seq, n_seq * s, D_B)
    seq_spec = pl.BlockSpec((None, n_seq * s, LANES), lambda hb, b: (b, 0, hb))
    bias_spec = pl.BlockSpec((None, None) + bias.shape[2:], lambda hb, b: (layer, hb, 0, 0, 0))
    return pl.pallas_call(
        functools.partial(_nat_kernel, rows=rows, n_seq=n_seq),
        grid=(N_HEAD_BLOCKS, bsz // n_seq),
        in_specs=[seq_spec, seq_spec, seq_spec, bias_spec],
        out_specs=seq_spec,
        out_shape=jax.ShapeDtypeStruct(grouped, BF16),
        scratch_shapes=[pltpu.VMEM((NAT_ROWS_PER_STEP, HEADS_PER_BLOCK * GRID_W, BAND), BF16),
                        pltpu.VMEM((NAT_ROWS_PER_STEP, HEADS_PER_BLOCK * GRID_W, LANES), F32)],
        compiler_params=_params(2),
        name="nat",
    )(q.reshape(grouped), k.reshape(grouped), v.reshape(grouped), bias).reshape(bsz, s, D_B)


def _merge_call(layer, h, ma, gb, yb, wbb, wo, g, wg, wu, wd, final_norm):
    t = h.shape[0]
    tm = _token_tile(t)
    consts = (wbb, wo, g, wg, wu, wd)
    const_specs = [_resident(c, layer) for c in consts]
    if final_norm is not None:
        consts += (final_norm,)
        const_specs.append(_resident(final_norm, 0))
    return pl.pallas_call(
        functools.partial(_merge_kernel, final=final_norm is not None),
        grid=(t // tm,),
        in_specs=[_token_spec(tm, D_MODEL)] * 4 + const_specs,
        out_specs=_token_spec(tm, D_MODEL),
        out_shape=jax.ShapeDtypeStruct((t, D_MODEL), F32),
        compiler_params=_params(),
        name="merge",
    )(h, ma, gb, yb, *consts)


def _dense_bias(rpb):
    pad = GRID_W - WIN_W
    padded = jnp.pad(rpb * LOG2E, ((0, 0), (0, 0), (pad, pad)), constant_values=MASKED)
    by_col = jnp.stack([padded[:, :, GRID_W - 1 - c:2 * GRID_W - 1 - c] for c in range(GRID_W)], axis=2)
    c = jnp.arange(GRID_W, dtype=jnp.int32)
    col_start = jnp.clip(c - WIN_W // 2, 0, GRID_W - WIN_W)
    kc = jnp.arange(GRID_W, dtype=jnp.int32)
    valid = (kc[None, :] >= col_start[:, None]) & (kc[None, :] < col_start[:, None] + WIN_W)
    by_col = jnp.where(valid[None, None], by_col, MASKED)
    tab = jnp.stack([by_col[:, WIN_H - 1 - d:2 * WIN_H - 1 - d] for d in range(WIN_H)], axis=1)
    tab = jnp.transpose(tab, (0, 1, 3, 2, 4)).reshape(N_HEAD_BLOCKS, HEADS_PER_BLOCK, WIN_H, GRID_W, BAND)
    return jnp.transpose(tab, (0, 2, 1, 3, 4)).reshape(N_HEAD_BLOCKS, WIN_H, HEADS_PER_BLOCK * GRID_W, BAND)


def _trunk(x, p, depth):
    bsz, s, _ = x.shape
    t = bsz * s
    h = x.reshape(t, D_MODEL)
    for l in range(depth):
        h = _ffn_call(l, h, p["ffn1_norm"], p["ffn1_w_gate"], p["ffn1_w_up"], p["ffn1_w_down"])
        ma, gb, q, k, v = _proj_call(l, h, p["mix_norm"], p["w_in"], p["b_gate"], p["sgu_ln_g"],
                                     p["sgu_ln_b"], p["sgu_w_s"], p["sgu_b_s"], p["w_branch_a"])
        shape3 = (bsz, s, D_B)
        yb = _nat_call(l, q.reshape(shape3), k.reshape(shape3), v.reshape(shape3), p["nat_bias"])
        h = _merge_call(l, h, ma, gb, yb.reshape(t, D_B), p["w_branch_b"], p["w_out"],
                        p["ffn2_norm"], p["ffn2_w_gate"], p["ffn2_w_up"], p["ffn2_w_down"],
                        p["final_norm"] if l == depth - 1 else None)
    return h.reshape(bsz, s, D_MODEL)


def kernel(x_prompt, x_sample, ffn1_norm, ffn1_w_gate, ffn1_w_up, ffn1_w_down, mix_norm, w_in, b_gate, sgu_ln_g, sgu_ln_b, sgu_w_s, sgu_b_s, nat_rpb, w_branch_a, w_branch_b, w_out, ffn2_norm, ffn2_w_gate, ffn2_w_up, ffn2_w_down, final_norm):
    depth = w_in.shape[0]
    row = lambda a: a.reshape(depth, 1, a.shape[-1])
    bs = jnp.repeat(jnp.transpose(sgu_b_s, (0, 2, 1)), GROUP_A, axis=-1)
    p = dict(
        ffn1_norm=row(ffn1_norm), ffn1_w_gate=ffn1_w_gate.astype(BF16), ffn1_w_up=ffn1_w_up.astype(BF16),
        ffn1_w_down=ffn1_w_down.astype(BF16), mix_norm=row(mix_norm), w_in=w_in.astype(BF16), b_gate=b_gate,
        sgu_ln_g=row(sgu_ln_g), sgu_ln_b=row(sgu_ln_b), sgu_w_s=sgu_w_s.astype(BF16), sgu_b_s=bs,
        nat_bias=jax.vmap(_dense_bias)(nat_rpb), w_branch_a=w_branch_a.astype(BF16),
        w_branch_b=w_branch_b.astype(BF16), w_out=w_out.astype(BF16), ffn2_norm=row(ffn2_norm),
        ffn2_w_gate=ffn2_w_gate.astype(BF16), ffn2_w_up=ffn2_w_up.astype(BF16),
        ffn2_w_down=ffn2_w_down.astype(BF16), final_norm=final_norm.reshape(1, 1, D_MODEL),
    )
    return (_trunk(x_prompt, p, depth), _trunk(x_sample, p, depth))
```

```python
import functools

import jax
import jax.numpy as jnp
from jax import lax
from jax.experimental import pallas as pl
from jax.experimental.pallas import tpu as pltpu

D_MODEL = 1024
GRID_W = 64
CHUNK = 128
D_A = 1024
N_GROUPS_A = 8
GROUP_A = D_A // N_GROUPS_A
N_HEADS_B = 16
HEAD_DIM_B = 64
D_B = N_HEADS_B * HEAD_DIM_B
WIN_H = 8
WIN_W = 16
D_FF = 2816
EPS = 1e-6

LANES = 128
HEADS_PER_BLOCK = LANES // HEAD_DIM_B
N_HEAD_BLOCKS = N_HEADS_B // HEADS_PER_BLOCK
BAND = WIN_H * GRID_W
NAT_ROWS_PER_STEP = 32
NAT_ROWS_PER_GRID_STEP = 128
NAT_LOOKAHEAD = 2
NAT_LEAD = 5
MXU_WIDTH = 256
_FF_SPLIT = MXU_WIDTH * ((D_FF // MXU_WIDTH + 1) // 2)
FF_RANGES = ((0, _FF_SPLIT), (_FF_SPLIT, D_FF))
TOKEN_TILE = 512
VMEM_LIMIT = 56 * 1024 * 1024
LOG2E = 1.4426950408889634
MASKED = -1e30

F32 = jnp.float32
BF16 = jnp.bfloat16


def _dot(a, b):
    return jnp.dot(a, b, preferred_element_type=F32)


def _rmsnorm(x, g):
    return x * lax.rsqrt(jnp.mean(x * x, axis=-1, keepdims=True) + EPS) * g


def _gelu(x):
    return 0.5 * x * (1.0 + lax.erf(x * (0.5 ** 0.5)))


def _swiglu_half_step(x, g_ref, wg_ref, wu_ref, wd_ref):
    n = _rmsnorm(x, g_ref[...]).astype(BF16)
    acc = None
    for lo, hi in FF_RANGES:
        sl = slice(lo, hi)
        gate = _dot(n, wg_ref[:, sl])
        up = _dot(n, wu_ref[:, sl])
        act = (jax.nn.silu(gate) * up).astype(BF16)
        part = _dot(act, wd_ref[sl, :])
        acc = part if acc is None else acc + part
    return x + 0.5 * acc


def _ffn_kernel(h_ref, g_ref, wg_ref, wu_ref, wd_ref, o_ref):
    o_ref[...] = _swiglu_half_step(h_ref[...], g_ref, wg_ref, wu_ref, wd_ref)


def _proj_kernel(h_ref, gn_ref, win_ref, bg_ref, lng_ref, lnb_ref, ws_ref, bs_ref, wba_ref,
                 ma_ref, gb_ref, q_ref, k_ref, v_ref, ya_ref):
    tm = h_ref.shape[0]
    n = _rmsnorm(h_ref[...], gn_ref[...]).astype(BF16)

    def proj(i, width):
        return _dot(n, win_ref[:, i:i + width])

    o_u, o_v, o_q, o_k, o_vv = 0, D_A, 2 * D_A, 2 * D_A + D_B, 2 * D_A + 2 * D_B
    o_ga, o_gb = 2 * D_A + 3 * D_B, 2 * D_A + 3 * D_B + D_MODEL

    vg = _gelu(proj(o_v, D_A))
    mu = jnp.mean(vg, axis=-1, keepdims=True)
    vc = vg - mu
    var = jnp.mean(vc * vc, axis=-1, keepdims=True)
    vn = (vc * lax.rsqrt(var + EPS) * lng_ref[...] + lnb_ref[...]).astype(BF16)
    u = _gelu(proj(o_u, D_A))

    q_ref[...] = (proj(o_q, D_B) * (HEAD_DIM_B ** -0.5 * LOG2E)).astype(BF16)
    k_ref[...] = proj(o_k, D_B).astype(BF16)
    v_ref[...] = proj(o_vv, D_B).astype(BF16)
    gb_ref[...] = jax.nn.sigmoid(proj(o_gb, D_MODEL) + bg_ref[1:2, :]).astype(BF16)

    for c in range(tm // CHUNK):
        rows = slice(c * CHUNK, (c + 1) * CHUNK)
        for g in range(N_GROUPS_A):
            cols = slice(g * GROUP_A, (g + 1) * GROUP_A)
            mixed = _dot(ws_ref[g], vn[rows, cols]) + bs_ref[:, cols]
            ya_ref[rows, cols] = (u[rows, cols] * mixed).astype(BF16)

    gate_a = jax.nn.sigmoid(proj(o_ga, D_MODEL) + bg_ref[0:1, :])
    ma_ref[...] = (gate_a * _dot(ya_ref[...], wba_ref[...])).astype(BF16)


def _nat_kernel(q_ref, k_ref, v_ref, bias_ref, o_ref, p_ring, *, rows, n_seq):
    lane = lax.broadcasted_iota(jnp.int32, (GRID_W, LANES), 1)
    first_head = lane < HEAD_DIM_B
    zero = jnp.zeros((GRID_W, LANES), BF16)
    ones = jnp.ones((BAND, LANES), BF16)
    total_rows = n_seq * rows

    def band_of(r):
        row_in_seq = lax.rem(r, rows)
        band_start = r - row_in_seq + jnp.clip(row_in_seq - WIN_H // 2, 0, rows - WIN_H)
        return r - band_start, pl.multiple_of(band_start * GRID_W, GRID_W)

    def scores(r, slot):
        r = jnp.minimum(r, total_rows - 1)
        d, k0 = band_of(r)
        qr = q_ref[pl.ds(pl.multiple_of(r * GRID_W, GRID_W), GRID_W), :]
        kb = k_ref[pl.ds(k0, BAND), :]
        qs = jnp.concatenate([jnp.where(first_head, qr, zero), jnp.where(first_head, zero, qr)], axis=0)
        s = lax.dot_general(qs, kb, (((1,), (1,)), ((), ())), preferred_element_type=F32)
        s = s + bias_ref[d]
        p = jnp.exp2(s - jnp.max(s, axis=-1, keepdims=True))
        p_ring[slot] = p.astype(BF16)

    def output(r, slot):
        _, k0 = band_of(r)
        o = _dot(p_ring[slot], jnp.concatenate([v_ref[pl.ds(k0, BAND), :], ones], axis=1))
        o = o[:, :LANES] / o[:, LANES:]
        q0 = pl.multiple_of(r * GRID_W, GRID_W)
        o_ref[pl.ds(q0, GRID_W), :] = jnp.where(first_head, o[:GRID_W], o[GRID_W:]).astype(BF16)

    for r in range(NAT_LOOKAHEAD):
        scores(r, r)

    def row_group(g, carry):
        base = g * NAT_ROWS_PER_STEP
        ahead = NAT_LOOKAHEAD + NAT_LEAD
        for i in range(NAT_LOOKAHEAD, ahead):
            scores(base + i, i % NAT_ROWS_PER_STEP)
        for i in range(NAT_ROWS_PER_STEP):
            output(base + i, i)
            if i + NAT_LEAD < NAT_ROWS_PER_STEP:
                scores(base + ahead + i, (ahead + i) % NAT_ROWS_PER_STEP)
        return carry

    lax.fori_loop(0, total_rows // NAT_ROWS_PER_STEP, row_group, 0)


def _merge_kernel(h_ref, ma_ref, gb_ref, yb_ref, wbb_ref, wo_ref, g_ref, wg_ref, wu_ref, wd_ref,
                  *rest, final):
    o_ref = rest[-1]
    merged = ma_ref[...].astype(F32) + gb_ref[...].astype(F32) * _dot(yb_ref[...], wbb_ref[...])
    h2 = h_ref[...] + _dot(merged.astype(BF16), wo_ref[...])
    h3 = _swiglu_half_step(h2, g_ref, wg_ref, wu_ref, wd_ref)
    if final:
        h3 = _rmsnorm(h3, rest[0][...])
    o_ref[...] = h3


def _resident(stacked, layer):
    shape = stacked.shape[1:]
    return pl.BlockSpec((None,) + shape, lambda *_: (layer,) + (0,) * len(shape), pipeline_mode=pl.Buffered(1))


def _token_spec(tm, width):
    return pl.BlockSpec((tm, width), lambda i: (i, 0))


def _params(n_axes=1):
    return pltpu.CompilerParams(dimension_semantics=("arbitrary",) * n_axes,
                                vmem_limit_bytes=VMEM_LIMIT)


def _token_tile(t):
    tm = min(TOKEN_TILE, t)
    assert t % tm == 0 and tm % CHUNK == 0
    return tm


def _ffn_call(layer, h, g, wg, wu, wd):
    t = h.shape[0]
    tm = _token_tile(t)
    return pl.pallas_call(
        _ffn_kernel,
        grid=(t // tm,),
        in_specs=[_token_spec(tm, D_MODEL)] + [_resident(c, layer) for c in (g, wg, wu, wd)],
        out_specs=_token_spec(tm, D_MODEL),
        out_shape=jax.ShapeDtypeStruct((t, D_MODEL), F32),
        compiler_params=_params(),
        name="ffn",
    )(h, g, wg, wu, wd)


def _proj_call(layer, h, gn, win, bg, lng, lnb, ws, bs, wba):
    t = h.shape[0]
    tm = _token_tile(t)
    consts = (gn, win, bg, lng, lnb, ws, bs, wba)
    act = jax.ShapeDtypeStruct((t, D_MODEL), BF16)
    return pl.pallas_call(
        _proj_kernel,
        grid=(t // tm,),
        in_specs=[_token_spec(tm, D_MODEL)] + [_resident(c, layer) for c in consts],
        out_specs=[_token_spec(tm, D_MODEL)] * 5,
        out_shape=[act] * 5,
        scratch_shapes=[pltpu.VMEM((tm, D_A), BF16)],
        compiler_params=_params(),
        name="proj",
    )(h, *consts)


def _nat_call(layer, q, k, v, bias):
    bsz, s, _ = q.shape
    rows = s // GRID_W
    assert s % GRID_W == 0 and rows >= WIN_H
    n_seq = max(n for n in range(1, bsz + 1) if bsz % n == 0 and n * rows <= max(rows, NAT_ROWS_PER_GRID_STEP))
    assert (n_seq * rows) % NAT_ROWS_PER_STEP == 0
    grouped = (bsz // n_seq, n_seq * s, D_B)
    seq_spec = pl.BlockSpec((None, n_seq * s, LANES), lambda hb, b: (b, 0, hb))
    bias_spec = pl.BlockSpec((None, None) + bias.shape[2:], lambda hb, b: (layer, hb, 0, 0, 0))
    return pl.pallas_call(
        functools.partial(_nat_kernel, rows=rows, n_seq=n_seq),
        grid=(N_HEAD_BLOCKS, bsz // n_seq),
        in_specs=[seq_spec, seq_spec, seq_spec, bias_spec],
        out_specs=seq_spec,
        out_shape=jax.ShapeDtypeStruct(grouped, BF16),
        scratch_shapes=[pltpu.VMEM((NAT_ROWS_PER_STEP, HEADS_PER_BLOCK * GRID_W, BAND), BF16)],
        compiler_params=_params(2),
        name="nat",
    )(q.reshape(grouped), k.reshape(grouped), v.reshape(grouped), bias).reshape(bsz, s, D_B)


def _merge_call(layer, h, ma, gb, yb, wbb, wo, g, wg, wu, wd, final_norm):
    t = h.shape[0]
    tm = _token_tile(t)
    consts = (wbb, wo, g, wg, wu, wd)
    const_specs = [_resident(c, layer) for c in consts]
    if final_norm is not None:
        consts += (final_norm,)
        const_specs.append(_resident(final_norm, 0))
    return pl.pallas_call(
        functools.partial(_merge_kernel, final=final_norm is not None),
        grid=(t // tm,),
        in_specs=[_token_spec(tm, D_MODEL)] * 4 + const_specs,
        out_specs=_token_spec(tm, D_MODEL),
        out_shape=jax.ShapeDtypeStruct((t, D_MODEL), F32),
        compiler_params=_params(),
        name="merge",
    )(h, ma, gb, yb, *consts)


def _dense_bias(rpb):
    pad = GRID_W - WIN_W
    padded = jnp.pad(rpb * LOG2E, ((0, 0), (0, 0), (pad, pad)), constant_values=MASKED)
    by_col = jnp.stack([padded[:, :, GRID_W - 1 - c:2 * GRID_W - 1 - c] for c in range(GRID_W)], axis=2)
    c = jnp.arange(GRID_W, dtype=jnp.int32)
    col_start = jnp.clip(c - WIN_W // 2, 0, GRID_W - WIN_W)
    kc = jnp.arange(GRID_W, dtype=jnp.int32)
    valid = (kc[None, :] >= col_start[:, None]) & (kc[None, :] < col_start[:, None] + WIN_W)
    by_col = jnp.where(valid[None, None], by_col, MASKED)
    tab = jnp.stack([by_col[:, WIN_H - 1 - d:2 * WIN_H - 1 - d] for d in range(WIN_H)], axis=1)
    tab = jnp.transpose(tab, (0, 1, 3, 2, 4)).reshape(N_HEAD_BLOCKS, HEADS_PER_BLOCK, WIN_H, GRID_W, BAND)
    return jnp.transpose(tab, (0, 2, 1, 3, 4)).reshape(N_HEAD_BLOCKS, WIN_H, HEADS_PER_BLOCK * GRID_W, BAND)


def _trunk(x, p, depth):
    bsz, s, _ = x.shape
    t = bsz * s
    h = x.reshape(t, D_MODEL)
    for l in range(depth):
        h = _ffn_call(l, h, p["ffn1_norm"], p["ffn1_w_gate"], p["ffn1_w_up"], p["ffn1_w_down"])
        ma, gb, q, k, v = _proj_call(l, h, p["mix_norm"], p["w_in"], p["b_gate"], p["sgu_ln_g"],
                                     p["sgu_ln_b"], p["sgu_w_s"], p["sgu_b_s"], p["w_branch_a"])
        shape3 = (bsz, s, D_B)
        yb = _nat_call(l, q.reshape(shape3), k.reshape(shape3), v.reshape(shape3), p["nat_bias"])
        h = _merge_call(l, h, ma, gb, yb.reshape(t, D_B), p["w_branch_b"], p["w_out"],
                        p["ffn2_norm"], p["ffn2_w_gate"], p["ffn2_w_up"], p["ffn2_w_down"],
                        p["final_norm"] if l == depth - 1 else None)
    return h.reshape(bsz, s, D_MODEL)


def kernel(x_prompt, x_sample, ffn1_norm, ffn1_w_gate, ffn1_w_up, ffn1_w_down, mix_norm, w_in, b_gate, sgu_ln_g, sgu_ln_b, sgu_w_s, sgu_b_s, nat_rpb, w_branch_a, w_branch_b, w_out, ffn2_norm, ffn2_w_gate, ffn2_w_up, ffn2_w_down, final_norm):
    depth = w_in.shape[0]
    row = lambda a: a.reshape(depth, 1, a.shape[-1])
    bs = jnp.repeat(jnp.transpose(sgu_b_s, (0, 2, 1)), GROUP_A, axis=-1)
    p = dict(
        ffn1_norm=row(ffn1_norm), ffn1_w_gate=ffn1_w_gate.astype(BF16), ffn1_w_up=ffn1_w_up.astype(BF16),
        ffn1_w_down=ffn1_w_down.astype(BF16), mix_norm=row(mix_norm), w_in=w_in.astype(BF16), b_gate=b_gate,
        sgu_ln_g=row(sgu_ln_g), sgu_ln_b=row(sgu_ln_b), sgu_w_s=sgu_w_s.astype(BF16), sgu_b_s=bs,
        nat_bias=jax.vmap(_dense_bias)(nat_rpb), w_branch_a=w_branch_a.astype(BF16),
        w_branch_b=w_branch_b.astype(BF16), w_out=w_out.astype(BF16), ffn2_norm=row(ffn2_norm),
        ffn2_w_gate=ffn2_w_gate.astype(BF16), ffn2_w_up=ffn2_w_up.astype(BF16),
        ffn2_w_down=ffn2_w_down.astype(BF16), final_norm=final_norm.reshape(1, 1, D_MODEL),
    )
    return (_trunk(x_prompt, p, depth), _trunk(x_sample, p, depth))
```

```python
import functools

import jax
import jax.numpy as jnp
from jax import lax
from jax.experimental import pallas as pl
from jax.experimental.pallas import tpu as pltpu

D_MODEL = 1024
GRID_W = 64
CHUNK = 128
D_A = 1024
N_GROUPS_A = 8
GROUP_A = D_A // N_GROUPS_A
N_HEADS_B = 16
HEAD_DIM_B = 64
D_B = N_HEADS_B * HEAD_DIM_B
WIN_H = 8
WIN_W = 16
D_FF = 2816
EPS = 1e-6

LANES = 128
HEADS_PER_BLOCK = LANES // HEAD_DIM_B
N_HEAD_BLOCKS = N_HEADS_B // HEADS_PER_BLOCK
BAND = WIN_H * GRID_W
NAT_ROWS_PER_STEP = 32
NAT_ROWS_PER_GRID_STEP = 128
NAT_LOOKAHEAD = 2
NAT_LEAD = 5
MXU_WIDTH = 256
_FF_SPLIT = MXU_WIDTH * ((D_FF // MXU_WIDTH + 1) // 2)
FF_RANGES = ((0, _FF_SPLIT), (_FF_SPLIT, D_FF))
TOKEN_TILE = 512
FFN_TILES_PER_STEP = 2
PROJ_TILES_PER_STEP = 2
MERGE_TILES_PER_STEP = 2
VMEM_LIMIT = 60 * 1024 * 1024
LOG2E = 1.4426950408889634
MASKED = -1e30

F32 = jnp.float32
BF16 = jnp.bfloat16


def _dot(a, b):
    return jnp.dot(a, b, preferred_element_type=F32)


def _rmsnorm(x, g):
    return x * lax.rsqrt(jnp.mean(x * x, axis=-1, keepdims=True) + EPS) * g


def _gelu(x):
    return 0.5 * x * (1.0 + lax.erf(x * (0.5 ** 0.5)))


def _swiglu_half_step(x, g_ref, wg_ref, wu_ref, wd_ref):
    n = _rmsnorm(x, g_ref[...]).astype(BF16)
    acc = None
    for lo, hi in FF_RANGES:
        sl = slice(lo, hi)
        gate = _dot(n, wg_ref[:, sl])
        up = _dot(n, wu_ref[:, sl])
        act = (jax.nn.silu(gate) * up).astype(BF16)
        part = _dot(act, wd_ref[sl, :])
        acc = part if acc is None else acc + part
    return x + 0.5 * acc


def _ffn_kernel(h_ref, g_ref, wg_ref, wu_ref, wd_ref, o_ref):
    for s in range(h_ref.shape[0] // TOKEN_TILE):
        rows = slice(s * TOKEN_TILE, (s + 1) * TOKEN_TILE)
        o_ref[rows, :] = _swiglu_half_step(h_ref[rows, :], g_ref, wg_ref, wu_ref, wd_ref)


def _proj_kernel(h_ref, gn_ref, win_ref, bg_ref, lng_ref, lnb_ref, ws_ref, bs_ref, wba_ref,
                 ma_ref, gb_ref, q_ref, k_ref, v_ref, ya_ref):
    for s in range(h_ref.shape[0] // TOKEN_TILE):
        tile = pl.ds(s * TOKEN_TILE, TOKEN_TILE)
        _proj_tile(h_ref.at[tile], gn_ref, win_ref, bg_ref, lng_ref, lnb_ref, ws_ref, bs_ref, wba_ref,
                   *(r.at[tile] for r in (ma_ref, gb_ref, q_ref, k_ref, v_ref, ya_ref)))


def _proj_tile(h_ref, gn_ref, win_ref, bg_ref, lng_ref, lnb_ref, ws_ref, bs_ref, wba_ref,
               ma_ref, gb_ref, q_ref, k_ref, v_ref, ya_ref):
    tm = h_ref.shape[0]
    n = _rmsnorm(h_ref[...], gn_ref[...]).astype(BF16)

    def proj(i, width):
        return _dot(n, win_ref[:, i:i + width])

    o_u, o_v, o_q, o_k, o_vv = 0, D_A, 2 * D_A, 2 * D_A + D_B, 2 * D_A + 2 * D_B
    o_ga, o_gb = 2 * D_A + 3 * D_B, 2 * D_A + 3 * D_B + D_MODEL

    vg = _gelu(proj(o_v, D_A))
    mu = jnp.mean(vg, axis=-1, keepdims=True)
    vc = vg - mu
    var = jnp.mean(vc * vc, axis=-1, keepdims=True)
    vn = (vc * lax.rsqrt(var + EPS) * lng_ref[...] + lnb_ref[...]).astype(BF16)
    u = _gelu(proj(o_u, D_A))

    q_ref[...] = (proj(o_q, D_B) * (HEAD_DIM_B ** -0.5 * LOG2E)).astype(BF16)
    k_ref[...] = proj(o_k, D_B).astype(BF16)
    v_ref[...] = proj(o_vv, D_B).astype(BF16)
    gb_ref[...] = jax.nn.sigmoid(proj(o_gb, D_MODEL) + bg_ref[1:2, :]).astype(BF16)

    for c in range(tm // CHUNK):
        rows = slice(c * CHUNK, (c + 1) * CHUNK)
        for g in range(N_GROUPS_A):
            cols = slice(g * GROUP_A, (g + 1) * GROUP_A)
            mixed = _dot(ws_ref[g], vn[rows, cols]) + bs_ref[:, cols]
            ya_ref[rows, cols] = (u[rows, cols] * mixed).astype(BF16)

    gate_a = jax.nn.sigmoid(proj(o_ga, D_MODEL) + bg_ref[0:1, :])
    ma_ref[...] = (gate_a * _dot(ya_ref[...], wba_ref[...])).astype(BF16)


def _nat_kernel(q_ref, k_ref, v_ref, bias_ref, o_ref, p_ring, *, rows, n_seq):
    lane = lax.broadcasted_iota(jnp.int32, (GRID_W, LANES), 1)
    first_head = lane < HEAD_DIM_B
    zero = jnp.zeros((GRID_W, LANES), BF16)
    ones = jnp.ones((BAND, LANES), BF16)
    total_rows = n_seq * rows

    def band_of(r):
        row_in_seq = lax.rem(r, rows)
        band_start = r - row_in_seq + jnp.clip(row_in_seq - WIN_H // 2, 0, rows - WIN_H)
        return r - band_start, pl.multiple_of(band_start * GRID_W, GRID_W)

    def scores(r, slot):
        r = jnp.minimum(r, total_rows - 1)
        d, k0 = band_of(r)
        qr = q_ref[pl.ds(pl.multiple_of(r * GRID_W, GRID_W), GRID_W), :]
        kb = k_ref[pl.ds(k0, BAND), :]
        qs = jnp.concatenate([jnp.where(first_head, qr, zero), jnp.where(first_head, zero, qr)], axis=0)
        s = lax.dot_general(qs, kb, (((1,), (1,)), ((), ())), preferred_element_type=F32)
        first = (WIN_H - 1) - d
        s = s + jnp.concatenate(
            [jnp.concatenate([bias_ref[h, first + j] for j in range(0, WIN_H, 2)], axis=1)
             for h in range(HEADS_PER_BLOCK)], axis=0)
        p = jnp.exp2(s - jnp.max(s, axis=-1, keepdims=True))
        p_ring[slot] = p.astype(BF16)

    def output(r, slot):
        _, k0 = band_of(r)
        o = _dot(p_ring[slot], jnp.concatenate([v_ref[pl.ds(k0, BAND), :], ones], axis=1))
        o = o[:, :LANES] / o[:, LANES:]
        q0 = pl.multiple_of(r * GRID_W, GRID_W)
        o_ref[pl.ds(q0, GRID_W), :] = jnp.where(first_head, o[:GRID_W], o[GRID_W:]).astype(BF16)

    for r in range(NAT_LOOKAHEAD):
        scores(r, r)

    def row_group(g, carry):
        base = g * NAT_ROWS_PER_STEP
        ahead = NAT_LOOKAHEAD + NAT_LEAD
        for i in range(NAT_LOOKAHEAD, ahead):
            scores(base + i, i % NAT_ROWS_PER_STEP)
        for i in range(NAT_ROWS_PER_STEP):
            output(base + i, i)
            if i + NAT_LEAD < NAT_ROWS_PER_STEP:
                scores(base + ahead + i, (ahead + i) % NAT_ROWS_PER_STEP)
        return carry

    lax.fori_loop(0, total_rows // NAT_ROWS_PER_STEP, row_group, 0)


def _merge_kernel(h_ref, ma_ref, gb_ref, yb_ref, wbb_ref, wo_ref, g_ref, wg_ref, wu_ref, wd_ref,
                  *rest, final):
    o_ref = rest[-1]
    for s in range(h_ref.shape[0] // TOKEN_TILE):
        rows = slice(s * TOKEN_TILE, (s + 1) * TOKEN_TILE)
        merged = ma_ref[rows, :].astype(F32) + gb_ref[rows, :].astype(F32) * _dot(yb_ref[rows, :], wbb_ref[...])
        h2 = h_ref[rows, :] + _dot(merged.astype(BF16), wo_ref[...])
        h3 = _swiglu_half_step(h2, g_ref, wg_ref, wu_ref, wd_ref)
        if final:
            h3 = _rmsnorm(h3, rest[0][...])
        o_ref[rows, :] = h3


def _resident(stacked, layer):
    shape = stacked.shape[1:]
    return pl.BlockSpec((None,) + shape, lambda *_: (layer,) + (0,) * len(shape), pipeline_mode=pl.Buffered(1))


def _token_spec(tm, width):
    return pl.BlockSpec((tm, width), lambda i: (i, 0))


def _params(n_axes=1):
    return pltpu.CompilerParams(dimension_semantics=("arbitrary",) * n_axes,
                                vmem_limit_bytes=VMEM_LIMIT)


def _token_tile(t):
    tm = min(TOKEN_TILE, t)
    assert t % tm == 0 and tm % CHUNK == 0
    return tm


def _ffn_call(layer, h, g, wg, wu, wd):
    t = h.shape[0]
    tm = FFN_TILES_PER_STEP * _token_tile(t)
    assert t % tm == 0
    return pl.pallas_call(
        _ffn_kernel,
        grid=(t // tm,),
        in_specs=[_token_spec(tm, D_MODEL)] + [_resident(c, layer) for c in (g, wg, wu, wd)],
        out_specs=_token_spec(tm, D_MODEL),
        out_shape=jax.ShapeDtypeStruct((t, D_MODEL), F32),
        compiler_params=_params(),
        name="ffn",
    )(h, g, wg, wu, wd)


def _proj_call(layer, h, gn, win, bg, lng, lnb, ws, bs, wba):
    t = h.shape[0]
    tm = PROJ_TILES_PER_STEP * _token_tile(t)
    assert t % tm == 0
    consts = (gn, win, bg, lng, lnb, ws, bs, wba)
    act = jax.ShapeDtypeStruct((t, D_MODEL), BF16)
    return pl.pallas_call(
        _proj_kernel,
        grid=(t // tm,),
        in_specs=[_token_spec(tm, D_MODEL)] + [_resident(c, layer) for c in consts],
        out_specs=[_token_spec(tm, D_MODEL)] * 5,
        out_shape=[act] * 5,
        scratch_shapes=[pltpu.VMEM((tm, D_A), BF16)],
        compiler_params=_params(),
        name="proj",
    )(h, *consts)


def _nat_call(layer, q, k, v, bias):
    bsz, s, _ = q.shape
    rows = s // GRID_W
    assert s % GRID_W == 0 and rows >= WIN_H
    n_seq = max(n for n in range(1, bsz + 1) if bsz % n == 0 and n * rows <= max(rows, NAT_ROWS_PER_GRID_STEP))
    assert (n_seq * rows) % NAT_ROWS_PER_STEP == 0
    grouped = (bsz // n_seq, n_seq * s, D_B)
    seq_spec = pl.BlockSpec((None, n_seq * s, LANES), lambda hb, b: (b, 0, hb))
    bias_spec = pl.BlockSpec((None, None) + bias.shape[2:], lambda hb, b: (layer, hb, 0, 0, 0, 0))
    return pl.pallas_call(
        functools.partial(_nat_kernel, rows=rows, n_seq=n_seq),
        grid=(N_HEAD_BLOCKS, bsz // n_seq),
        in_specs=[seq_spec, seq_spec, seq_spec, bias_spec],
        out_specs=seq_spec,
        out_shape=jax.ShapeDtypeStruct(grouped, BF16),
        scratch_shapes=[pltpu.VMEM((NAT_ROWS_PER_STEP, HEADS_PER_BLOCK * GRID_W, BAND), BF16)],
        compiler_params=_params(2),
        name="nat",
    )(q.reshape(grouped), k.reshape(grouped), v.reshape(grouped), bias).reshape(bsz, s, D_B)


def _merge_call(layer, h, ma, gb, yb, wbb, wo, g, wg, wu, wd, final_norm):
    t = h.shape[0]
    tm = MERGE_TILES_PER_STEP * _token_tile(t)
    assert t % tm == 0
    consts = (wbb, wo, g, wg, wu, wd)
    const_specs = [_resident(c, layer) for c in consts]
    if final_norm is not None:
        consts += (final_norm,)
        const_specs.append(_resident(final_norm, 0))
    return pl.pallas_call(
        functools.partial(_merge_kernel, final=final_norm is not None),
        grid=(t // tm,),
        in_specs=[_token_spec(tm, D_MODEL)] * 4 + const_specs,
        out_specs=_token_spec(tm, D_MODEL),
        out_shape=jax.ShapeDtypeStruct((t, D_MODEL), F32),
        compiler_params=_params(),
        name="merge",
    )(h, ma, gb, yb, *consts)


def _dense_bias(rpb):
    pad = GRID_W - WIN_W
    padded = jnp.pad(rpb * LOG2E, ((0, 0), (0, 0), (pad, pad)), constant_values=MASKED)
    by_col = jnp.stack([padded[:, :, GRID_W - 1 - c:2 * GRID_W - 1 - c] for c in range(GRID_W)], axis=2)
    c = jnp.arange(GRID_W, dtype=jnp.int32)
    col_start = jnp.clip(c - WIN_W // 2, 0, GRID_W - WIN_W)
    kc = jnp.arange(GRID_W, dtype=jnp.int32)
    valid = (kc[None, :] >= col_start[:, None]) & (kc[None, :] < col_start[:, None] + WIN_W)
    by_col = jnp.where(valid[None, None], by_col, MASKED)
    pairs = jnp.concatenate([by_col[:, :-1], by_col[:, 1:]], axis=-1)
    return pairs.reshape((N_HEAD_BLOCKS, HEADS_PER_BLOCK) + pairs.shape[1:])


def _trunk(x, p, depth):
    bsz, s, _ = x.shape
    t = bsz * s
    h = x.reshape(t, D_MODEL)
    for l in range(depth):
        h = _ffn_call(l, h, p["ffn1_norm"], p["ffn1_w_gate"], p["ffn1_w_up"], p["ffn1_w_down"])
        ma, gb, q, k, v = _proj_call(l, h, p["mix_norm"], p["w_in"], p["b_gate"], p["sgu_ln_g"],
                                     p["sgu_ln_b"], p["sgu_w_s"], p["sgu_b_s"], p["w_branch_a"])
        shape3 = (bsz, s, D_B)
        yb = _nat_call(l, q.reshape(shape3), k.reshape(shape3), v.reshape(shape3), p["nat_bias"])
        h = _merge_call(l, h, ma, gb, yb.reshape(t, D_B), p["w_branch_b"], p["w_out"],
                        p["ffn2_norm"], p["ffn2_w_gate"], p["ffn2_w_up"], p["ffn2_w_down"],
                        p["final_norm"] if l == depth - 1 else None)
    return h.reshape(bsz, s, D_MODEL)


def kernel(x_prompt, x_sample, ffn1_norm, ffn1_w_gate, ffn1_w_up, ffn1_w_down, mix_norm, w_in, b_gate, sgu_ln_g, sgu_ln_b, sgu_w_s, sgu_b_s, nat_rpb, w_branch_a, w_branch_b, w_out, ffn2_norm, ffn2_w_gate, ffn2_w_up, ffn2_w_down, final_norm):
    depth = w_in.shape[0]
    row = lambda a: a.reshape(depth, 1, a.shape[-1])
    bs = jnp.repeat(jnp.transpose(sgu_b_s, (0, 2, 1)), GROUP_A, axis=-1)
    p = dict(
        ffn1_norm=row(ffn1_norm), ffn1_w_gate=ffn1_w_gate.astype(BF16), ffn1_w_up=ffn1_w_up.astype(BF16),
        ffn1_w_down=ffn1_w_down.astype(BF16), mix_norm=row(mix_norm), w_in=w_in.astype(BF16), b_gate=b_gate,
        sgu_ln_g=row(sgu_ln_g), sgu_ln_b=row(sgu_ln_b), sgu_w_s=sgu_w_s.astype(BF16), sgu_b_s=bs,
        nat_bias=jax.vmap(_dense_bias)(nat_rpb), w_branch_a=w_branch_a.astype(BF16),
        w_branch_b=w_branch_b.astype(BF16), w_out=w_out.astype(BF16), ffn2_norm=row(ffn2_norm),
        ffn2_w_gate=ffn2_w_gate.astype(BF16), ffn2_w_up=ffn2_w_up.astype(BF16),
        ffn2_w_down=ffn2_w_down.astype(BF16), final_norm=final_norm.reshape(1, 1, D_MODEL),
    )
    return (_trunk(x_prompt, p, depth), _trunk(x_sample, p, depth))
```

```python
import functools

import jax
import jax.numpy as jnp
from jax import lax
from jax.experimental import pallas as pl
from jax.experimental.pallas import tpu as pltpu

D_MODEL = 1024
GRID_W = 64
CHUNK = 128
D_A = 1024
N_GROUPS_A = 8
GROUP_A = D_A // N_GROUPS_A
N_HEADS_B = 16
HEAD_DIM_B = 64
D_B = N_HEADS_B * HEAD_DIM_B
WIN_H = 8
WIN_W = 16
D_FF = 2816
EPS = 1e-6

LANES = 128
HEADS_PER_BLOCK = LANES // HEAD_DIM_B
N_HEAD_BLOCKS = N_HEADS_B // HEADS_PER_BLOCK
BAND = WIN_H * GRID_W
NAT_ROWS_PER_STEP = 32
NAT_ROWS_PER_GRID_STEP = 128
NAT_LOOKAHEAD = 2
NAT_LEAD = 5
MXU_WIDTH = 256
_FF_SPLIT = MXU_WIDTH * ((D_FF // MXU_WIDTH + 1) // 2)
FF_RANGES = ((0, _FF_SPLIT), (_FF_SPLIT, D_FF))
TOKEN_TILE = 512
FFN_TILES_PER_STEP = 2
PROJ_TILES_PER_STEP = 2
MERGE_TILES_PER_STEP = 2
VMEM_LIMIT = 60 * 1024 * 1024
LOG2E = 1.4426950408889634
MASKED = -1e30

F32 = jnp.float32
BF16 = jnp.bfloat16


def _dot(a, b):
    return jnp.dot(a, b, preferred_element_type=F32)


def _rmsnorm(x, g):
    return x * lax.rsqrt(jnp.mean(x * x, axis=-1, keepdims=True) + EPS) * g


def _gelu(x):
    return 0.5 * x * (1.0 + lax.erf(x * (0.5 ** 0.5)))


def _swiglu_half_step(x, g_ref, wg_ref, wu_ref, wd_ref):
    n = _rmsnorm(x, g_ref[...]).astype(BF16)
    acc = None
    for lo, hi in FF_RANGES:
        sl = slice(lo, hi)
        gate = _dot(n, wg_ref[:, sl])
        up = _dot(n, wu_ref[:, sl])
        act = (jax.nn.silu(gate) * up).astype(BF16)
        part = _dot(act, wd_ref[sl, :])
        acc = part if acc is None else acc + part
    return x + 0.5 * acc


def _ffn_kernel(h_ref, g_ref, wg_ref, wu_ref, wd_ref, o_ref):
    for s in range(h_ref.shape[0] // TOKEN_TILE):
        rows = slice(s * TOKEN_TILE, (s + 1) * TOKEN_TILE)
        o_ref[rows, :] = _swiglu_half_step(h_ref[rows, :], g_ref, wg_ref, wu_ref, wd_ref)


def _proj_kernel(h_ref, gn_ref, win_ref, bg_ref, lng_ref, lnb_ref, ws_ref, bs_ref, wba_ref,
                 ma_ref, gb_ref, q_ref, k_ref, v_ref, ya_ref):
    for s in range(h_ref.shape[0] // TOKEN_TILE):
        tile = pl.ds(s * TOKEN_TILE, TOKEN_TILE)
        _proj_tile(h_ref.at[tile], gn_ref, win_ref, bg_ref, lng_ref, lnb_ref, ws_ref, bs_ref, wba_ref,
                   *(r.at[tile] for r in (ma_ref, gb_ref, q_ref, k_ref, v_ref, ya_ref)))


def _proj_tile(h_ref, gn_ref, win_ref, bg_ref, lng_ref, lnb_ref, ws_ref, bs_ref, wba_ref,
               ma_ref, gb_ref, q_ref, k_ref, v_ref, ya_ref):
    tm = h_ref.shape[0]
    n = _rmsnorm(h_ref[...], gn_ref[...]).astype(BF16)

    def proj(i, width):
        return _dot(n, win_ref[:, i:i + width])

    o_u, o_v, o_q, o_k, o_vv = 0, D_A, 2 * D_A, 2 * D_A + D_B, 2 * D_A + 2 * D_B
    o_ga, o_gb = 2 * D_A + 3 * D_B, 2 * D_A + 3 * D_B + D_MODEL

    vg = _gelu(proj(o_v, D_A))
    mu = jnp.mean(vg, axis=-1, keepdims=True)
    vc = vg - mu
    var = jnp.mean(vc * vc, axis=-1, keepdims=True)
    vn = (vc * lax.rsqrt(var + EPS) * lng_ref[...] + lnb_ref[...]).astype(BF16)
    u = _gelu(proj(o_u, D_A))

    q_ref[...] = (proj(o_q, D_B) * (HEAD_DIM_B ** -0.5 * LOG2E)).astype(BF16)
    k_ref[...] = proj(o_k, D_B).astype(BF16)
    v_ref[...] = proj(o_vv, D_B).astype(BF16)
    gb_ref[...] = jax.nn.sigmoid(proj(o_gb, D_MODEL) + bg_ref[1:2, :]).astype(BF16)

    for c in range(tm // CHUNK):
        rows = slice(c * CHUNK, (c + 1) * CHUNK)
        for g in range(N_GROUPS_A):
            cols = slice(g * GROUP_A, (g + 1) * GROUP_A)
            mixed = _dot(ws_ref[g], vn[rows, cols]) + bs_ref[:, cols]
            ya_ref[rows, cols] = (u[rows, cols] * mixed).astype(BF16)

    gate_a = jax.nn.sigmoid(proj(o_ga, D_MODEL) + bg_ref[0:1, :])
    ma_ref[...] = (gate_a * _dot(ya_ref[...], wba_ref[...])).astype(BF16)


def _nat_kernel(q_ref, k_ref, v_ref, bias_ref, o_ref, p_ring, *, rows, n_seq):
    lane = lax.broadcasted_iota(jnp.int32, (GRID_W, LANES), 1)
    first_head = lane < HEAD_DIM_B
    zero = jnp.zeros((GRID_W, LANES), BF16)
    ones = jnp.ones((BAND, LANES), BF16)
    total_rows = n_seq * rows

    def band_of(r):
        row_in_seq = lax.rem(r, rows)
        band_start = r - row_in_seq + jnp.clip(row_in_seq - WIN_H // 2, 0, rows - WIN_H)
        return r - band_start, pl.multiple_of(band_start * GRID_W, GRID_W)

    def scores(r, slot):
        r = jnp.minimum(r, total_rows - 1)
        d, k0 = band_of(r)
        qr = q_ref[pl.ds(pl.multiple_of(r * GRID_W, GRID_W), GRID_W), :]
        kb = k_ref[pl.ds(k0, BAND), :]
        qs = jnp.concatenate([jnp.where(first_head, qr, zero), jnp.where(first_head, zero, qr)], axis=0)
        s = lax.dot_general(qs, kb, (((1,), (1,)), ((), ())), preferred_element_type=F32)
        first = (WIN_H - 1) - d
        s = s + jnp.concatenate(
            [jnp.concatenate([bias_ref[h, first + j] for j in range(0, WIN_H, 2)], axis=1)
             for h in range(HEADS_PER_BLOCK)], axis=0)
        p = jnp.exp2(s - jnp.max(s, axis=-1, keepdims=True))
        p_ring[slot] = p.astype(BF16)

    def output(r, slot):
        _, k0 = band_of(r)
        o = _dot(p_ring[slot], jnp.concatenate([v_ref[pl.ds(k0, BAND), :], ones], axis=1))
        o = o[:, :LANES] / o[:, LANES:]
        q0 = pl.multiple_of(r * GRID_W, GRID_W)
        o_ref[pl.ds(q0, GRID_W), :] = jnp.where(first_head, o[:GRID_W], o[GRID_W:]).astype(BF16)

    for r in range(NAT_LOOKAHEAD):
        scores(r, r)

    def row_group(g, carry):
        base = g * NAT_ROWS_PER_STEP
        ahead = NAT_LOOKAHEAD + NAT_LEAD
        for i in range(NAT_LOOKAHEAD, ahead):
            scores(base + i, i % NAT_ROWS_PER_STEP)
        for i in range(NAT_ROWS_PER_STEP):
            output(base + i, i)
            if i + NAT_LEAD < NAT_ROWS_PER_STEP:
                scores(base + ahead + i, (ahead + i) % NAT_ROWS_PER_STEP)
        return carry

    lax.fori_loop(0, total_rows // NAT_ROWS_PER_STEP, row_group, 0)


def _merge_kernel(h_ref, ma_ref, gb_ref, yb_ref, wbb_ref, wo_ref, g_ref, wg_ref, wu_ref, wd_ref,
                  *rest, final):
    o_ref = rest[-1]
    tiles = [slice(s * TOKEN_TILE, (s + 1) * TOKEN_TILE) for s in range(h_ref.shape[0] // TOKEN_TILE)]
    branch_b = [_dot(yb_ref[rows, :], wbb_ref[...]) for rows in tiles]
    merged = [(ma_ref[rows, :].astype(F32) + gb_ref[rows, :].astype(F32) * b).astype(BF16)
              for rows, b in zip(tiles, branch_b)]
    h2 = [h_ref[rows, :] + _dot(m, wo_ref[...]) for rows, m in zip(tiles, merged)]
    for rows, x in zip(tiles, h2):
        h3 = _swiglu_half_step(x, g_ref, wg_ref, wu_ref, wd_ref)
        if final:
            h3 = _rmsnorm(h3, rest[0][...])
        o_ref[rows, :] = h3


def _resident(stacked, layer):
    shape = stacked.shape[1:]
    return pl.BlockSpec((None,) + shape, lambda *_: (layer,) + (0,) * len(shape), pipeline_mode=pl.Buffered(1))


def _token_spec(tm, width):
    return pl.BlockSpec((tm, width), lambda i: (i, 0))


def _params(n_axes=1):
    return pltpu.CompilerParams(dimension_semantics=("arbitrary",) * n_axes,
                                vmem_limit_bytes=VMEM_LIMIT)


def _token_tile(t):
    tm = min(TOKEN_TILE, t)
    assert t % tm == 0 and tm % CHUNK == 0
    return tm


def _ffn_call(layer, h, g, wg, wu, wd):
    t = h.shape[0]
    tm = FFN_TILES_PER_STEP * _token_tile(t)
    assert t % tm == 0
    return pl.pallas_call(
        _ffn_kernel,
        grid=(t // tm,),
        in_specs=[_token_spec(tm, D_MODEL)] + [_resident(c, layer) for c in (g, wg, wu, wd)],
        out_specs=_token_spec(tm, D_MODEL),
        out_shape=jax.ShapeDtypeStruct((t, D_MODEL), F32),
        compiler_params=_params(),
        name="ffn",
    )(h, g, wg, wu, wd)


def _proj_call(layer, h, gn, win, bg, lng, lnb, ws, bs, wba):
    t = h.shape[0]
    tm = PROJ_TILES_PER_STEP * _token_tile(t)
    assert t % tm == 0
    consts = (gn, win, bg, lng, lnb, ws, bs, wba)
    act = jax.ShapeDtypeStruct((t, D_MODEL), BF16)
    return pl.pallas_call(
        _proj_kernel,
        grid=(t // tm,),
        in_specs=[_token_spec(tm, D_MODEL)] + [_resident(c, layer) for c in consts],
        out_specs=[_token_spec(tm, D_MODEL)] * 5,
        out_shape=[act] * 5,
        scratch_shapes=[pltpu.VMEM((tm, D_A), BF16)],
        compiler_params=_params(),
        name="proj",
    )(h, *consts)


def _nat_call(layer, q, k, v, bias):
    bsz, s, _ = q.shape
    rows = s // GRID_W
    assert s % GRID_W == 0 and rows >= WIN_H
    n_seq = max(n for n in range(1, bsz + 1) if bsz % n == 0 and n * rows <= max(rows, NAT_ROWS_PER_GRID_STEP))
    assert (n_seq * rows) % NAT_ROWS_PER_STEP == 0
    grouped = (bsz // n_seq, n_seq * s, D_B)
    seq_spec = pl.BlockSpec((None, n_seq * s, LANES), lambda hb, b: (b, 0, hb))
    bias_spec = pl.BlockSpec((None, None) + bias.shape[2:], lambda hb, b: (layer, hb, 0, 0, 0, 0))
    return pl.pallas_call(
        functools.partial(_nat_kernel, rows=rows, n_seq=n_seq),
        grid=(N_HEAD_BLOCKS, bsz // n_seq),
        in_specs=[seq_spec, seq_spec, seq_spec, bias_spec],
        out_specs=seq_spec,
        out_shape=jax.ShapeDtypeStruct(grouped, BF16),
        scratch_shapes=[pltpu.VMEM((NAT_ROWS_PER_STEP, HEADS_PER_BLOCK * GRID_W, BAND), BF16)],
        compiler_params=_params(2),
        name="nat",
    )(q.reshape(grouped), k.reshape(grouped), v.reshape(grouped), bias).reshape(bsz, s, D_B)


def _merge_call(layer, h, ma, gb, yb, wbb, wo, g, wg, wu, wd, final_norm):
    t = h.shape[0]
    tm = MERGE_TILES_PER_STEP * _token_tile(t)
    assert t % tm == 0
    consts = (wbb, wo, g, wg, wu, wd)
    const_specs = [_resident(c, layer) for c in consts]
    if final_norm is not None:
        consts += (final_norm,)
        const_specs.append(_resident(final_norm, 0))
    return pl.pallas_call(
        functools.partial(_merge_kernel, final=final_norm is not None),
        grid=(t // tm,),
        in_specs=[_token_spec(tm, D_MODEL)] * 4 + const_specs,
        out_specs=_token_spec(tm, D_MODEL),
        out_shape=jax.ShapeDtypeStruct((t, D_MODEL), F32),
        compiler_params=_params(),
        name="merge",
    )(h, ma, gb, yb, *consts)


def _dense_bias(rpb):
    pad = GRID_W - WIN_W
    padded = jnp.pad(rpb * LOG2E, ((0, 0), (0, 0), (pad, pad)), constant_values=MASKED)
    by_col = jnp.stack([padded[:, :, GRID_W - 1 - c:2 * GRID_W - 1 - c] for c in range(GRID_W)], axis=2)
    c = jnp.arange(GRID_W, dtype=jnp.int32)
    col_start = jnp.clip(c - WIN_W // 2, 0, GRID_W - WIN_W)
    kc = jnp.arange(GRID_W, dtype=jnp.int32)
    valid = (kc[None, :] >= col_start[:, None]) & (kc[None, :] < col_start[:, None] + WIN_W)
    by_col = jnp.where(valid[None, None], by_col, MASKED)
    pairs = jnp.concatenate([by_col[:, :-1], by_col[:, 1:]], axis=-1)
    return pairs.reshape((N_HEAD_BLOCKS, HEADS_PER_BLOCK) + pairs.shape[1:])


def _trunk(x, p, depth):
    bsz, s, _ = x.shape
    t = bsz * s
    h = x.reshape(t, D_MODEL)
    for l in range(depth):
        h = _ffn_call(l, h, p["ffn1_norm"], p["ffn1_w_gate"], p["ffn1_w_up"], p["ffn1_w_down"])
        ma, gb, q, k, v = _proj_call(l, h, p["mix_norm"], p["w_in"], p["b_gate"], p["sgu_ln_g"],
                                     p["sgu_ln_b"], p["sgu_w_s"], p["sgu_b_s"], p["w_branch_a"])
        shape3 = (bsz, s, D_B)
        yb = _nat_call(l, q.reshape(shape3), k.reshape(shape3), v.reshape(shape3), p["nat_bias"])
        h = _merge_call(l, h, ma, gb, yb.reshape(t, D_B), p["w_branch_b"], p["w_out"],
                        p["ffn2_norm"], p["ffn2_w_gate"], p["ffn2_w_up"], p["ffn2_w_down"],
                        p["final_norm"] if l == depth - 1 else None)
    return h.reshape(bsz, s, D_MODEL)


def kernel(x_prompt, x_sample, ffn1_norm, ffn1_w_gate, ffn1_w_up, ffn1_w_down, mix_norm, w_in, b_gate, sgu_ln_g, sgu_ln_b, sgu_w_s, sgu_b_s, nat_rpb, w_branch_a, w_branch_b, w_out, ffn2_norm, ffn2_w_gate, ffn2_w_up, ffn2_w_down, final_norm):
    depth = w_in.shape[0]
    row = lambda a: a.reshape(depth, 1, a.shape[-1])
    bs = jnp.repeat(jnp.transpose(sgu_b_s, (0, 2, 1)), GROUP_A, axis=-1)
    p = dict(
        ffn1_norm=row(ffn1_norm), ffn1_w_gate=ffn1_w_gate.astype(BF16), ffn1_w_up=ffn1_w_up.astype(BF16),
        ffn1_w_down=ffn1_w_down.astype(BF16), mix_norm=row(mix_norm), w_in=w_in.astype(BF16), b_gate=b_gate,
        sgu_ln_g=row(sgu_ln_g), sgu_ln_b=row(sgu_ln_b), sgu_w_s=sgu_w_s.astype(BF16), sgu_b_s=bs,
        nat_bias=jax.vmap(_dense_bias)(nat_rpb), w_branch_a=w_branch_a.astype(BF16),
        w_branch_b=w_branch_b.astype(BF16), w_out=w_out.astype(BF16), ffn2_norm=row(ffn2_norm),
        ffn2_w_gate=ffn2_w_gate.astype(BF16), ffn2_w_up=ffn2_w_up.astype(BF16),
        ffn2_w_down=ffn2_w_down.astype(BF16), final_norm=final_norm.reshape(1, 1, D_MODEL),
    )
    return (_trunk(x_prompt, p, depth), _trunk(x_sample, p, depth))
```

```python
import functools

import jax
import jax.numpy as jnp
from jax import lax
from jax.experimental import pallas as pl
from jax.experimental.pallas import tpu as pltpu

D_MODEL = 1024
GRID_W = 64
CHUNK = 128
D_A = 1024
N_GROUPS_A = 8
GROUP_A = D_A // N_GROUPS_A
N_HEADS_B = 16
HEAD_DIM_B = 64
D_B = N_HEADS_B * HEAD_DIM_B
WIN_H = 8
WIN_W = 16
D_FF = 2816
EPS = 1e-6

LANES = 128
HEADS_PER_BLOCK = LANES // HEAD_DIM_B
N_HEAD_BLOCKS = N_HEADS_B // HEADS_PER_BLOCK
BAND = WIN_H * GRID_W
NAT_ROWS_PER_STEP = 64
NAT_ROWS_PER_GRID_STEP = 256
NAT_LOOKAHEAD = 2
NAT_LEAD = 5
MXU_WIDTH = 256
_FF_SPLIT = MXU_WIDTH * ((D_FF // MXU_WIDTH + 1) // 2)
FF_RANGES = ((0, _FF_SPLIT), (_FF_SPLIT, D_FF))
TOKEN_TILE = 512
FFN_TILES_PER_STEP = 2
PROJ_TILES_PER_STEP = 2
MERGE_TILES_PER_STEP = 2
VMEM_LIMIT = 60 * 1024 * 1024
LOG2E = 1.4426950408889634
MASKED = -1e30

F32 = jnp.float32
BF16 = jnp.bfloat16


def _dot(a, b):
    return jnp.dot(a, b, preferred_element_type=F32)


def _rmsnorm(x, g):
    return x * lax.rsqrt(jnp.mean(x * x, axis=-1, keepdims=True) + EPS) * g


def _gelu(x):
    return 0.5 * x * (1.0 + lax.erf(x * (0.5 ** 0.5)))


def _swiglu_half_step(x, g_ref, wg_ref, wu_ref, wd_ref):
    n = _rmsnorm(x, g_ref[...]).astype(BF16)
    acc = None
    for lo, hi in FF_RANGES:
        sl = slice(lo, hi)
        gate = _dot(n, wg_ref[:, sl])
        up = _dot(n, wu_ref[:, sl])
        act = (jax.nn.silu(gate) * up).astype(BF16)
        part = _dot(act, wd_ref[sl, :])
        acc = part if acc is None else acc + part
    return x + 0.5 * acc


def _ffn_kernel(h_ref, g_ref, wg_ref, wu_ref, wd_ref, o_ref):
    for s in range(h_ref.shape[0] // TOKEN_TILE):
        rows = slice(s * TOKEN_TILE, (s + 1) * TOKEN_TILE)
        o_ref[rows, :] = _swiglu_half_step(h_ref[rows, :], g_ref, wg_ref, wu_ref, wd_ref)


def _proj_kernel(h_ref, gn_ref, win_ref, bg_ref, lng_ref, lnb_ref, ws_ref, bs_ref, wba_ref,
                 ma_ref, gb_ref, q_ref, k_ref, v_ref, ya_ref):
    for s in range(h_ref.shape[0] // TOKEN_TILE):
        tile = pl.ds(s * TOKEN_TILE, TOKEN_TILE)
        _proj_tile(h_ref.at[tile], gn_ref, win_ref, bg_ref, lng_ref, lnb_ref, ws_ref, bs_ref, wba_ref,
                   *(r.at[tile] for r in (ma_ref, gb_ref, q_ref, k_ref, v_ref, ya_ref)))


def _proj_tile(h_ref, gn_ref, win_ref, bg_ref, lng_ref, lnb_ref, ws_ref, bs_ref, wba_ref,
               ma_ref, gb_ref, q_ref, k_ref, v_ref, ya_ref):
    tm = h_ref.shape[0]
    n = _rmsnorm(h_ref[...], gn_ref[...]).astype(BF16)

    def proj(i, width):
        return _dot(n, win_ref[:, i:i + width])

    o_u, o_v, o_q, o_k, o_vv = 0, D_A, 2 * D_A, 2 * D_A + D_B, 2 * D_A + 2 * D_B
    o_ga, o_gb = 2 * D_A + 3 * D_B, 2 * D_A + 3 * D_B + D_MODEL

    vg = _gelu(proj(o_v, D_A))
    mu = jnp.mean(vg, axis=-1, keepdims=True)
    vc = vg - mu
    var = jnp.mean(vc * vc, axis=-1, keepdims=True)
    vn = (vc * lax.rsqrt(var + EPS) * lng_ref[...] + lnb_ref[...]).astype(BF16)
    u = _gelu(proj(o_u, D_A))

    q_ref[...] = (proj(o_q, D_B) * (HEAD_DIM_B ** -0.5 * LOG2E)).astype(BF16)
    k_ref[...] = proj(o_k, D_B).astype(BF16)
    v_ref[...] = proj(o_vv, D_B).astype(BF16)
    gb_ref[...] = jax.nn.sigmoid(proj(o_gb, D_MODEL) + bg_ref[1:2, :]).astype(BF16)

    for c in range(tm // CHUNK):
        rows = slice(c * CHUNK, (c + 1) * CHUNK)
        for g in range(N_GROUPS_A):
            cols = slice(g * GROUP_A, (g + 1) * GROUP_A)
            mixed = _dot(ws_ref[g], vn[rows, cols]) + bs_ref[:, cols]
            ya_ref[rows, cols] = (u[rows, cols] * mixed).astype(BF16)

    gate_a = jax.nn.sigmoid(proj(o_ga, D_MODEL) + bg_ref[0:1, :])
    ma_ref[...] = (gate_a * _dot(ya_ref[...], wba_ref[...])).astype(BF16)


def _nat_kernel(q_ref, k_ref, v_ref, bias_ref, o_ref, p_ring, *, rows, n_seq):
    lane = lax.broadcasted_iota(jnp.int32, (GRID_W, LANES), 1)
    first_head = lane < HEAD_DIM_B
    zero = jnp.zeros((GRID_W, LANES), BF16)
    ones = jnp.ones((BAND, LANES), BF16)
    total_rows = n_seq * rows

    def band_of(r):
        row_in_seq = lax.rem(r, rows)
        band_start = r - row_in_seq + jnp.clip(row_in_seq - WIN_H // 2, 0, rows - WIN_H)
        return r - band_start, pl.multiple_of(band_start * GRID_W, GRID_W)

    def scores(r, slot):
        r = jnp.minimum(r, total_rows - 1)
        d, k0 = band_of(r)
        qr = q_ref[pl.ds(pl.multiple_of(r * GRID_W, GRID_W), GRID_W), :]
        kb = k_ref[pl.ds(k0, BAND), :]
        qs = jnp.concatenate([jnp.where(first_head, qr, zero), jnp.where(first_head, zero, qr)], axis=0)
        s = lax.dot_general(qs, kb, (((1,), (1,)), ((), ())), preferred_element_type=F32)
        first = (WIN_H - 1) - d
        s = s + jnp.concatenate(
            [jnp.concatenate([bias_ref[h, first + j] for j in range(0, WIN_H, 2)], axis=1)
             for h in range(HEADS_PER_BLOCK)], axis=0)
        p = jnp.exp2(s - jnp.max(s, axis=-1, keepdims=True))
        p_ring[slot] = p.astype(BF16)

    def output(r, slot):
        _, k0 = band_of(r)
        o = _dot(p_ring[slot], jnp.concatenate([v_ref[pl.ds(k0, BAND), :], ones], axis=1))
        o = o[:, :LANES] / o[:, LANES:]
        q0 = pl.multiple_of(r * GRID_W, GRID_W)
        o_ref[pl.ds(q0, GRID_W), :] = jnp.where(first_head, o[:GRID_W], o[GRID_W:]).astype(BF16)

    for r in range(NAT_LOOKAHEAD):
        scores(r, r)

    def row_group(g, carry):
        base = g * NAT_ROWS_PER_STEP
        ahead = NAT_LOOKAHEAD + NAT_LEAD
        for i in range(NAT_LOOKAHEAD, ahead):
            scores(base + i, i % NAT_ROWS_PER_STEP)
        for i in range(NAT_ROWS_PER_STEP):
            output(base + i, i)
            if i + NAT_LEAD < NAT_ROWS_PER_STEP:
                scores(base + ahead + i, (ahead + i) % NAT_ROWS_PER_STEP)
        return carry

    lax.fori_loop(0, total_rows // NAT_ROWS_PER_STEP, row_group, 0)


def _merge_kernel(h_ref, ma_ref, gb_ref, yb_ref, wbb_ref, wo_ref, g_ref, wg_ref, wu_ref, wd_ref,
                  *rest, final):
    o_ref = rest[-1]
    tiles = [slice(s * TOKEN_TILE, (s + 1) * TOKEN_TILE) for s in range(h_ref.shape[0] // TOKEN_TILE)]
    branch_b = [_dot(yb_ref[rows, :], wbb_ref[...]) for rows in tiles]
    merged = [(ma_ref[rows, :].astype(F32) + gb_ref[rows, :].astype(F32) * b).astype(BF16)
              for rows, b in zip(tiles, branch_b)]
    h2 = [h_ref[rows, :] + _dot(m, wo_ref[...]) for rows, m in zip(tiles, merged)]
    for rows, x in zip(tiles, h2):
        h3 = _swiglu_half_step(x, g_ref, wg_ref, wu_ref, wd_ref)
        if final:
            h3 = _rmsnorm(h3, rest[0][...])
        o_ref[rows, :] = h3


def _resident(stacked, layer):
    shape = stacked.shape[1:]
    return pl.BlockSpec((None,) + shape, lambda *_: (layer,) + (0,) * len(shape), pipeline_mode=pl.Buffered(1))


def _token_spec(tm, width):
    return pl.BlockSpec((tm, width), lambda i: (i, 0))


def _params(n_axes=1):
    return pltpu.CompilerParams(dimension_semantics=("arbitrary",) * n_axes,
                                vmem_limit_bytes=VMEM_LIMIT)


def _token_tile(t):
    tm = min(TOKEN_TILE, t)
    assert t % tm == 0 and tm % CHUNK == 0
    return tm


def _ffn_call(layer, h, g, wg, wu, wd):
    t = h.shape[0]
    tm = FFN_TILES_PER_STEP * _token_tile(t)
    assert t % tm == 0
    return pl.pallas_call(
        _ffn_kernel,
        grid=(t // tm,),
        in_specs=[_token_spec(tm, D_MODEL)] + [_resident(c, layer) for c in (g, wg, wu, wd)],
        out_specs=_token_spec(tm, D_MODEL),
        out_shape=jax.ShapeDtypeStruct((t, D_MODEL), F32),
        compiler_params=_params(),
        name="ffn",
    )(h, g, wg, wu, wd)


def _proj_call(layer, h, gn, win, bg, lng, lnb, ws, bs, wba):
    t = h.shape[0]
    tm = PROJ_TILES_PER_STEP * _token_tile(t)
    assert t % tm == 0
    consts = (gn, win, bg, lng, lnb, ws, bs, wba)
    act = jax.ShapeDtypeStruct((t, D_MODEL), BF16)
    return pl.pallas_call(
        _proj_kernel,
        grid=(t // tm,),
        in_specs=[_token_spec(tm, D_MODEL)] + [_resident(c, layer) for c in consts],
        out_specs=[_token_spec(tm, D_MODEL)] * 5,
        out_shape=[act] * 5,
        scratch_shapes=[pltpu.VMEM((tm, D_A), BF16)],
        compiler_params=_params(),
        name="proj",
    )(h, *consts)


def _nat_call(layer, q, k, v, bias):
    bsz, s, _ = q.shape
    rows = s // GRID_W
    assert s % GRID_W == 0 and rows >= WIN_H
    n_seq = max(n for n in range(1, bsz + 1) if bsz % n == 0 and n * rows <= max(rows, NAT_ROWS_PER_GRID_STEP))
    assert (n_seq * rows) % NAT_ROWS_PER_STEP == 0
    grouped = (bsz // n_seq, n_seq * s, D_B)
    seq_spec = pl.BlockSpec((None, n_seq * s, LANES), lambda hb, b: (b, 0, hb))
    bias_spec = pl.BlockSpec((None, None) + bias.shape[2:], lambda hb, b: (layer, hb, 0, 0, 0, 0))
    return pl.pallas_call(
        functools.partial(_nat_kernel, rows=rows, n_seq=n_seq),
        grid=(N_HEAD_BLOCKS, bsz // n_seq),
        in_specs=[seq_spec, seq_spec, seq_spec, bias_spec],
        out_specs=seq_spec,
        out_shape=jax.ShapeDtypeStruct(grouped, BF16),
        scratch_shapes=[pltpu.VMEM((NAT_ROWS_PER_STEP, HEADS_PER_BLOCK * GRID_W, BAND), BF16)],
        compiler_params=_params(2),
        name="nat",
    )(q.reshape(grouped), k.reshape(grouped), v.reshape(grouped), bias).reshape(bsz, s, D_B)


def _merge_call(layer, h, ma, gb, yb, wbb, wo, g, wg, wu, wd, final_norm):
    t = h.shape[0]
    tm = MERGE_TILES_PER_STEP * _token_tile(t)
    assert t % tm == 0
    consts = (wbb, wo, g, wg, wu, wd)
    const_specs = [_resident(c, layer) for c in consts]
    if final_norm is not None:
        consts += (final_norm,)
        const_specs.append(_resident(final_norm, 0))
    return pl.pallas_call(
        functools.partial(_merge_kernel, final=final_norm is not None),
        grid=(t // tm,),
        in_specs=[_token_spec(tm, D_MODEL)] * 4 + const_specs,
        out_specs=_token_spec(tm, D_MODEL),
        out_shape=jax.ShapeDtypeStruct((t, D_MODEL), F32),
        compiler_params=_params(),
        name="merge",
    )(h, ma, gb, yb, *consts)


def _dense_bias(rpb):
    pad = GRID_W - WIN_W
    padded = jnp.pad(rpb * LOG2E, ((0, 0), (0, 0), (pad, pad)), constant_values=MASKED)
    by_col = jnp.stack([padded[:, :, GRID_W - 1 - c:2 * GRID_W - 1 - c] for c in range(GRID_W)], axis=2)
    c = jnp.arange(GRID_W, dtype=jnp.int32)
    col_start = jnp.clip(c - WIN_W // 2, 0, GRID_W - WIN_W)
    kc = jnp.arange(GRID_W, dtype=jnp.int32)
    valid = (kc[None, :] >= col_start[:, None]) & (kc[None, :] < col_start[:, None] + WIN_W)
    by_col = jnp.where(valid[None, None], by_col, MASKED)
    pairs = jnp.concatenate([by_col[:, :-1], by_col[:, 1:]], axis=-1)
    return pairs.reshape((N_HEAD_BLOCKS, HEADS_PER_BLOCK) + pairs.shape[1:])


def _trunk(x, p, depth):
    bsz, s, _ = x.shape
    t = bsz * s
    h = x.reshape(t, D_MODEL)
    for l in range(depth):
        h = _ffn_call(l, h, p["ffn1_norm"], p["ffn1_w_gate"], p["ffn1_w_up"], p["ffn1_w_down"])
        ma, gb, q, k, v = _proj_call(l, h, p["mix_norm"], p["w_in"], p["b_gate"], p["sgu_ln_g"],
                                     p["sgu_ln_b"], p["sgu_w_s"], p["sgu_b_s"], p["w_branch_a"])
        shape3 = (bsz, s, D_B)
        yb = _nat_call(l, q.reshape(shape3), k.reshape(shape3), v.reshape(shape3), p["nat_bias"])
        h = _merge_call(l, h, ma, gb, yb.reshape(t, D_B), p["w_branch_b"], p["w_out"],
                        p["ffn2_norm"], p["ffn2_w_gate"], p["ffn2_w_up"], p["ffn2_w_down"],
                        p["final_norm"] if l == depth - 1 else None)
    return h.reshape(bsz, s, D_MODEL)


def kernel(x_prompt, x_sample, ffn1_norm, ffn1_w_gate, ffn1_w_up, ffn1_w_down, mix_norm, w_in, b_gate, sgu_ln_g, sgu_ln_b, sgu_w_s, sgu_b_s, nat_rpb, w_branch_a, w_branch_b, w_out, ffn2_norm, ffn2_w_gate, ffn2_w_up, ffn2_w_down, final_norm):
    depth = w_in.shape[0]
    row = lambda a: a.reshape(depth, 1, a.shape[-1])
    bs = jnp.repeat(jnp.transpose(sgu_b_s, (0, 2, 1)), GROUP_A, axis=-1)
    p = dict(
        ffn1_norm=row(ffn1_norm), ffn1_w_gate=ffn1_w_gate.astype(BF16), ffn1_w_up=ffn1_w_up.astype(BF16),
        ffn1_w_down=ffn1_w_down.astype(BF16), mix_norm=row(mix_norm), w_in=w_in.astype(BF16), b_gate=b_gate,
        sgu_ln_g=row(sgu_ln_g), sgu_ln_b=row(sgu_ln_b), sgu_w_s=sgu_w_s.astype(BF16), sgu_b_s=bs,
        nat_bias=jax.vmap(_dense_bias)(nat_rpb), w_branch_a=w_branch_a.astype(BF16),
        w_branch_b=w_branch_b.astype(BF16), w_out=w_out.astype(BF16), ffn2_norm=row(ffn2_norm),
        ffn2_w_gate=ffn2_w_gate.astype(BF16), ffn2_w_up=ffn2_w_up.astype(BF16),
        ffn2_w_down=ffn2_w_down.astype(BF16), final_norm=final_norm.reshape(1, 1, D_MODEL),
    )
    return (_trunk(x_prompt, p, depth), _trunk(x_sample, p, depth))
```
